```python
import math
import jax, jax.numpy as jnp
from jax import lax
import numpy as np

D_MODEL = 1024
BATCH = 4
SEQ = 4096
DEPTH = 2
DEC_BATCH = 32
DEC_SEQ = 1
PAST_LEN = 16384
PAGE_SIZE = 128

A_HEADS = 8
A_KV_HEADS = 2
A_HEAD_DIM = 64
IDX_HEADS = 4
IDX_DIM = 64
IDX_TOPK_MAX = 256
B_HEADS = 4
B_QK_DIM = 64
B_V_DIM = 2 * B_QK_DIM
MEM_TOKENS = 256
M_HEADS = 4
M_HEAD_DIM = 128
D_FF = 2816
N_BRANCH = 3
ROPE_THETA = 10000.0
RMS_EPS = 1e-6
Q_BLOCK = 128

SPLITS = (A_HEADS * A_HEAD_DIM, A_KV_HEADS * A_HEAD_DIM, A_KV_HEADS * A_HEAD_DIM,
          IDX_HEADS * IDX_DIM, IDX_DIM, IDX_HEADS,
          B_HEADS * 2 * B_QK_DIM, B_HEADS * 2 * B_QK_DIM, B_HEADS * B_V_DIM,
          M_HEADS * M_HEAD_DIM, N_BRANCH * D_MODEL)
D_IN = sum(SPLITS)

kernel_name = 'hybrid_dsa_diffattn_memory_step'


def rms_norm(x, g):
    xf = x.astype(jnp.float32)
    y = xf * lax.rsqrt(jnp.mean(xf * xf, axis=-1, keepdims=True) + RMS_EPS)
    return (y * g.astype(jnp.float32)).astype(x.dtype)


def rope(x, pos):
    half = x.shape[-1] // 2
    inv = ROPE_THETA ** (-jnp.arange(half, dtype=jnp.float32) / half)
    ang = pos.astype(jnp.float32)[:, None] * inv[None, :]
    cos = jnp.cos(ang)[None, :, None, :]
    sin = jnp.sin(ang)[None, :, None, :]
    xf = x.astype(jnp.float32)
    x1, x2 = xf[..., :half], xf[..., half:]
    return jnp.concatenate([x1 * cos - x2 * sin, x2 * cos + x1 * sin], axis=-1).astype(x.dtype)


def swiglu(h, w_in, w_out):
    g, u = jnp.split(h @ w_in, 2, axis=-1)
    return (jax.nn.silu(g) * u) @ w_out


def sweep_queries(fn, q_arrays, q_pos):
    t = q_pos.shape[0]
    if t <= Q_BLOCK or t % Q_BLOCK != 0:
        return fn(q_arrays, q_pos)
    nb = t // Q_BLOCK
    blocks = tuple(jnp.moveaxis(a.reshape(a.shape[0], nb, Q_BLOCK, *a.shape[2:]), 1, 0) for a in q_arrays)
    out = lax.map(lambda xs: fn(xs[0], xs[1]), (blocks, q_pos.reshape(nb, Q_BLOCK)))
    out = jnp.moveaxis(out, 0, 1)
    return out.reshape(out.shape[0], t, out.shape[-1])


def dsa_block(qs, q_pos, k, v, ki, top_k):
    q, qi, wi = qs
    tk = k.shape[1]
    k_pos = jnp.arange(tk, dtype=jnp.int32)
    idx_logits = jnp.einsum('bthd,bsd->btsh', qi.astype(jnp.float32), ki.astype(jnp.float32)) * (IDX_DIM ** -0.5)
    score = jnp.einsum('btsh,bth->bts', jax.nn.relu(idx_logits), wi.astype(jnp.float32))
    causal = k_pos[None, :] <= q_pos[:, None]
    score = jnp.where(causal[None], score, -jnp.inf)
    _, sel = lax.top_k(score, top_k)
    valid = sel <= q_pos[None, :, None]
    gather = jax.vmap(lambda a, i: a[i])
    k_sel = gather(k, sel).astype(jnp.float32)
    v_sel = gather(v, sel).astype(jnp.float32)
    b, t = q.shape[:2]
    qg = q.reshape(b, t, A_KV_HEADS, A_HEADS // A_KV_HEADS, A_HEAD_DIM).astype(jnp.float32)
    logits = jnp.einsum('btgrd,btkgd->btgrk', qg, k_sel) * (A_HEAD_DIM ** -0.5)
    logits = jnp.where(valid[:, :, None, None, :], logits, -jnp.inf)
    p = jax.nn.softmax(logits, axis=-1)
    out = jnp.einsum('btgrk,btkgd->btgrd', p, v_sel)
    return out.reshape(b, t, A_HEADS * A_HEAD_DIM).astype(q.dtype)


def diff_block(qs, q_pos, k, v, lam):
    q = qs[0]
    tk = k.shape[1]
    k_pos = jnp.arange(tk, dtype=jnp.int32)
    causal = k_pos[None, :] <= q_pos[:, None]
    logits = jnp.einsum('bqhcd,bkhcd->bhcqk', q.astype(jnp.float32), k.astype(jnp.float32)) * (B_QK_DIM ** -0.5)
    logits = jnp.where(causal[None, None, None], logits, -jnp.inf)
    p = jax.nn.softmax(logits, axis=-1)
    attn = p[:, :, 0] - lam * p[:, :, 1]
    out = jnp.einsum('bhqk,bkhd->bqhd', attn, v.astype(jnp.float32))
    b, t = q.shape[:2]
    return out.reshape(b, t, B_HEADS * B_V_DIM).astype(q.dtype)


def mem_attend(q, mk, mv):
    logits = jnp.einsum('bthd,bmhd->bhtm', q.astype(jnp.float32), mk.astype(jnp.float32)) * (M_HEAD_DIM ** -0.5)
    p = jax.nn.softmax(logits, axis=-1)
    out = jnp.einsum('bhtm,bmhd->bthd', p, mv.astype(jnp.float32))
    b, t = q.shape[:2]
    return out.reshape(b, t, M_HEADS * M_HEAD_DIM).astype(q.dtype)


def split_columns(x):
    points = []
    acc = 0
    for s in SPLITS[:-1]:
        acc += s
        points.append(acc)
    return jnp.split(x, points, axis=-1)


def mixer_block(h, pos, past, mem_k, mem_v, p, layer_idx, top_k):
    b, t, _ = h.shape
    aq, ak, av, iq, ik, iw, bq, bk, bv, mq, gates = split_columns(h @ p['w_in'])
    aq = rope(aq.reshape(b, t, A_HEADS, A_HEAD_DIM), pos)
    ak = rope(ak.reshape(b, t, A_KV_HEADS, A_HEAD_DIM), pos)
    av = av.reshape(b, t, A_KV_HEADS, A_HEAD_DIM)
    iq = rope(iq.reshape(b, t, IDX_HEADS, IDX_DIM), pos)
    ik = rope(ik.reshape(b, t, 1, IDX_DIM), pos)[:, :, 0]
    bq = rope(bq.reshape(b, t, B_HEADS * 2, B_QK_DIM), pos).reshape(b, t, B_HEADS, 2, B_QK_DIM)
    bk = rope(bk.reshape(b, t, B_HEADS * 2, B_QK_DIM), pos).reshape(b, t, B_HEADS, 2 * B_QK_DIM)
    bv = bv.reshape(b, t, B_HEADS, B_V_DIM)
    new_rows = (ak, av, ik, bk, bv)
    if past is None:
        keys = new_rows
    else:
        keys = tuple(jnp.concatenate([pa.astype(n.dtype), n], axis=1) for pa, n in zip(past, new_rows))
    k_a, v_a, k_i, k_b, v_b = keys

    o_a = sweep_queries(lambda qs, qp: dsa_block(qs, qp, k_a, v_a, k_i, top_k), (aq, iq, iw), pos)

    lambda_init = 0.8 - 0.6 * math.exp(-0.3 * layer_idx)
    lam = (jnp.exp(jnp.sum(p['lambda_q1'].astype(jnp.float32) * p['lambda_k1'].astype(jnp.float32)))
           - jnp.exp(jnp.sum(p['lambda_q2'].astype(jnp.float32) * p['lambda_k2'].astype(jnp.float32)))
           + lambda_init)
    k_b4 = k_b.reshape(b, k_b.shape[1], B_HEADS, 2, B_QK_DIM)
    o_b = sweep_queries(lambda qs, qp: diff_block(qs, qp, k_b4, v_b, lam), (bq,), pos)
    o_b = (rms_norm(o_b.reshape(b, t, B_HEADS, B_V_DIM), p['subln_gain']) * (1.0 - lambda_init)).reshape(b, t, B_HEADS * B_V_DIM)

    o_m = mem_attend(mq.reshape(b, t, M_HEADS, M_HEAD_DIM), mem_k, mem_v)

    g = jax.nn.sigmoid(gates.astype(jnp.float32)).reshape(b, t, N_BRANCH, D_MODEL).astype(h.dtype)
    merged = (g[:, :, 0] * (o_a @ p['w_proj_a'])
              + g[:, :, 1] * (o_b @ p['w_proj_b'])
              + g[:, :, 2] * (o_m @ p['w_proj_m']))
    return merged @ p['w_out'], new_rows


def decoder_layer(x, pos, past, mem_k, mem_v, p, layer_idx, top_k):
    x = x + 0.5 * swiglu(rms_norm(x, p['norm_ffn1']), p['ffn1_w_in'], p['ffn1_w_out'])
    mix, rows = mixer_block(rms_norm(x, p['norm_mix']), pos, past, mem_k, mem_v, p, layer_idx, top_k)
    x = x + mix
    x = x + 0.5 * swiglu(rms_norm(x, p['norm_ffn2']), p['ffn2_w_in'], p['ffn2_w_out'])
    return x, rows


def gather_pages(cache, page_table, layer_idx):
    g = cache[page_table, layer_idx]
    return g.reshape(g.shape[0], g.shape[1] * g.shape[2], *g.shape[3:])


def setup_inputs(seed: int = 0) -> dict:
    key = jax.random.key(seed)
    ks = iter(list(jax.random.split(key, 48)))
    f32 = jnp.float32

    def nrm(shape, scale=1.0):
        return jax.random.normal(next(ks), shape, f32) * scale

    def gain(shape):
        return 1.0 + 0.02 * jax.random.normal(next(ks), shape, f32)

    n_pages = PAST_LEN // PAGE_SIZE
    n_used = DEC_BATCH * n_pages
    n_pool = n_used + (n_used + 3) // 4
    page_table = jax.random.permutation(next(ks), n_pool)[:n_used].reshape(DEC_BATCH, n_pages).astype(jnp.int32)
    a_w = A_HEADS * A_HEAD_DIM
    b_w = B_HEADS * B_V_DIM
    m_w = M_HEADS * M_HEAD_DIM
    return {
        'x_prompt': nrm((BATCH, SEQ, D_MODEL)),
        'x_sample': nrm((DEC_BATCH, DEC_SEQ, D_MODEL)),
        'cache_a_k': nrm((n_pool, DEPTH, PAGE_SIZE, A_KV_HEADS, A_HEAD_DIM)),
        'cache_a_v': nrm((n_pool, DEPTH, PAGE_SIZE, A_KV_HEADS, A_HEAD_DIM)),
        'cache_idx_k': nrm((n_pool, DEPTH, PAGE_SIZE, IDX_DIM)),
        'cache_b_k': nrm((n_pool, DEPTH, PAGE_SIZE, B_HEADS, 2 * B_QK_DIM)),
        'cache_b_v': nrm((n_pool, DEPTH, PAGE_SIZE, B_HEADS, B_V_DIM)),
        'cache_mem_k': nrm((DEC_BATCH, DEPTH, MEM_TOKENS, M_HEADS, M_HEAD_DIM)),
        'cache_mem_v': nrm((DEC_BATCH, DEPTH, MEM_TOKENS, M_HEADS, M_HEAD_DIM)),
        'page_table': page_table,
        'mem_prompt': nrm((BATCH, MEM_TOKENS, D_MODEL)),
        'norm_ffn1': gain((DEPTH, D_MODEL)),
        'ffn1_w_in': nrm((DEPTH, D_MODEL, 2 * D_FF), D_MODEL ** -0.5),
        'ffn1_w_out': nrm((DEPTH, D_FF, D_MODEL), D_FF ** -0.5),
        'norm_mix': gain((DEPTH, D_MODEL)),
        'w_in': nrm((DEPTH, D_MODEL, D_IN), D_MODEL ** -0.5),
        'lambda_q1': nrm((DEPTH, B_QK_DIM), 0.1),
        'lambda_k1': nrm((DEPTH, B_QK_DIM), 0.1),
        'lambda_q2': nrm((DEPTH, B_QK_DIM), 0.1),
        'lambda_k2': nrm((DEPTH, B_QK_DIM), 0.1),
        'subln_gain': gain((DEPTH, B_V_DIM)),
        'norm_mem': gain((DEPTH, D_MODEL)),
        'w_mem_kv': nrm((DEPTH, D_MODEL, 2 * m_w), D_MODEL ** -0.5),
        'w_proj_a': nrm((DEPTH, a_w, D_MODEL), a_w ** -0.5),
        'w_proj_b': nrm((DEPTH, b_w, D_MODEL), b_w ** -0.5),
        'w_proj_m': nrm((DEPTH, m_w, D_MODEL), m_w ** -0.5),
        'w_out': nrm((DEPTH, D_MODEL, D_MODEL), D_MODEL ** -0.5),
        'norm_ffn2': gain((DEPTH, D_MODEL)),
        'ffn2_w_in': nrm((DEPTH, D_MODEL, 2 * D_FF), D_MODEL ** -0.5),
        'ffn2_w_out': nrm((DEPTH, D_FF, D_MODEL), D_FF ** -0.5),
        'final_norm': gain((D_MODEL,)),
    }


def reference(x_prompt, x_sample, cache_a_k, cache_a_v, cache_idx_k, cache_b_k, cache_b_v,
              cache_mem_k, cache_mem_v, page_table, mem_prompt,
              norm_ffn1, ffn1_w_in, ffn1_w_out, norm_mix, w_in,
              lambda_q1, lambda_k1, lambda_q2, lambda_k2, subln_gain,
              norm_mem, w_mem_kv, w_proj_a, w_proj_b, w_proj_m, w_out,
              norm_ffn2, ffn2_w_in, ffn2_w_out, final_norm):
    seq = x_prompt.shape[1]
    dec_seq = x_sample.shape[1]
    past_len = page_table.shape[1] * cache_a_k.shape[2]
    top_k_prompt = min(IDX_TOPK_MAX, seq // 4)
    top_k_sample = min(IDX_TOPK_MAX, (past_len + dec_seq) // 4)
    pos_prompt = jnp.arange(seq, dtype=jnp.int32)
    pos_sample = past_len + jnp.arange(dec_seq, dtype=jnp.int32)
    bp = x_prompt.shape[0]

    xp, xs = x_prompt, x_sample
    rows_p, rows_s, mem_rows = [], [], []
    for l in range(DEPTH):
        p = {
            'norm_ffn1': norm_ffn1[l], 'ffn1_w_in': ffn1_w_in[l], 'ffn1_w_out': ffn1_w_out[l],
            'norm_mix': norm_mix[l], 'w_in': w_in[l],
            'lambda_q1': lambda_q1[l], 'lambda_k1': lambda_k1[l],
            'lambda_q2': lambda_q2[l], 'lambda_k2': lambda_k2[l], 'subln_gain': subln_gain[l],
            'w_proj_a': w_proj_a[l], 'w_proj_b': w_proj_b[l], 'w_proj_m': w_proj_m[l], 'w_out': w_out[l],
            'norm_ffn2': norm_ffn2[l], 'ffn2_w_in': ffn2_w_in[l], 'ffn2_w_out': ffn2_w_out[l],
        }
        mk, mv = jnp.split(rms_norm(mem_prompt, norm_mem[l]) @ w_mem_kv[l], 2, axis=-1)
        mk = mk.reshape(bp, MEM_TOKENS, M_HEADS, M_HEAD_DIM)
        mv = mv.reshape(bp, MEM_TOKENS, M_HEADS, M_HEAD_DIM)
        mem_rows.append((mk, mv))
        xp, rp = decoder_layer(xp, pos_prompt, None, mk, mv, p, l, top_k_prompt)
        rows_p.append(rp)
        past = tuple(gather_pages(c, page_table, l) for c in (cache_a_k, cache_a_v, cache_idx_k, cache_b_k, cache_b_v))
        xs, rs = decoder_layer(xs, pos_sample, past, cache_mem_k[:, l].astype(xs.dtype), cache_mem_v[:, l].astype(xs.dtype), p, l, top_k_sample)
        rows_s.append(rs)

    def stack(rows, i):
        return jnp.stack([r[i] for r in rows], axis=1)

    return (rms_norm(xp, final_norm), rms_norm(xs, final_norm),
            stack(rows_p, 0), stack(rows_p, 1), stack(rows_p, 2), stack(rows_p, 3), stack(rows_p, 4),
            stack(mem_rows, 0), stack(mem_rows, 1),
            stack(rows_s, 0), stack(rows_s, 1), stack(rows_s, 2), stack(rows_s, 3), stack(rows_s, 4))
```

```python
import functools
import math

import jax
import jax.numpy as jnp
from jax import lax
from jax.experimental import pallas as pl
from jax.experimental.pallas import tpu as pltpu

F32 = jnp.float32
BF16 = jnp.bfloat16
I32 = jnp.int32

A_HEADS, A_KV_HEADS, A_HEAD_DIM = 8, 2, 64
IDX_HEADS, IDX_DIM, IDX_TOPK_MAX = 4, 64, 256
B_HEADS, B_QK_DIM, B_V_DIM = 4, 64, 128
MEM_TOKENS, M_HEADS, M_HEAD_DIM = 256, 4, 128
N_BRANCH = 3
ROPE_THETA = 10000.0
RMS_EPS = 1e-6

LANES = 128
V7X_VMEM_LIMIT_BYTES = 60000 * 1024

NEG_BIG = -1e30
INT_MIN = -(2 ** 31)
KEY_NEG_INF = -(2 ** 31) + 0x7FFFFF

PROJ_COLS = dict(aq=(0, 1024), ak=(1024, 128), av=(1152, 128), ik=(1280, 128), iq=(1408, 512),
                 bq=(1920, 1024), bk=(2944, 512), bv=(3456, 512))
PROJ_WIDTH = 3968


def _cparams(sem):
    return pltpu.CompilerParams(dimension_semantics=sem, vmem_limit_bytes=V7X_VMEM_LIMIT_BYTES)


def _resident(shape):
    zeros = (0,) * len(shape)
    return pl.BlockSpec(shape, lambda *_: zeros, pipeline_mode=pl.Buffered(1))


def _rms(x, g):
    return x * lax.rsqrt(jnp.mean(x * x, axis=-1, keepdims=True) + RMS_EPS) * g


def _dot(a, b):
    return jnp.dot(a, b, preferred_element_type=F32)


def _dot_nt(a, b):
    return lax.dot_general(a, b, (((1,), (1,)), ((), ())), preferred_element_type=F32)


def _row_tile(rows, target=512):
    t = min(rows, target)
    assert rows % t == 0 and t % 8 == 0, (rows, t)
    return t


def _ffn_body(*refs, d_ff, chunk, final):
    if final:
        x_ref, g_ref, wi_ref, wo_ref, fg_ref, o_ref, of_ref = refs
    else:
        x_ref, g_ref, wi_ref, wo_ref, o_ref = refs
    x = x_ref[...]
    h = _rms(x, g_ref[...]).astype(BF16)
    acc = jnp.zeros_like(x)
    for c in range(d_ff // chunk):
        gate = _dot(h, wi_ref[:, c * chunk:(c + 1) * chunk])
        up = _dot(h, wi_ref[:, d_ff + c * chunk:d_ff + (c + 1) * chunk])
        act = (gate * jax.nn.sigmoid(gate) * up).astype(BF16)
        acc = acc + _dot(act, wo_ref[c * chunk:(c + 1) * chunk, :])
    y = x + 0.5 * acc
    o_ref[...] = y
    if final:
        of_ref[...] = _rms(y, fg_ref[...])


def _ffn(x, g, w_in, w_out, final_g=None):
    rows, d = x.shape
    d_ff = w_out.shape[0]
    chunk = d_ff // 2 if (d_ff // 2) % LANES == 0 else d_ff
    tm = _row_tile(rows)
    final = final_g is not None
    row_spec = pl.BlockSpec((tm, d), lambda i: (i, 0))
    in_specs = [row_spec, _resident((1, d)), _resident(w_in.shape), _resident(w_out.shape)]
    args = [x, g.reshape(1, d), w_in, w_out]
    out_shape = [jax.ShapeDtypeStruct((rows, d), F32)]
    out_specs = [row_spec]
    if final:
        in_specs.append(_resident((1, d)))
        args.append(final_g.reshape(1, d))
        out_shape.append(jax.ShapeDtypeStruct((rows, d), F32))
        out_specs.append(row_spec)
    out = pl.pallas_call(
        functools.partial(_ffn_body, d_ff=d_ff, chunk=chunk, final=final),
        grid=(rows // tm,), in_specs=in_specs, out_specs=out_specs, out_shape=out_shape,
        compiler_params=_cparams(("arbitrary",)), name="ffn_final" if final else "ffn",
    )(*args)
    return out if final else out[0]


def _rope_tables(pos):
    half = A_HEAD_DIM // 2
    inv = ROPE_THETA ** (-jnp.arange(half, dtype=F32) / half)
    ang = pos.astype(F32)[:, None] * inv[None, :]
    cos, sin = jnp.cos(ang), jnp.sin(ang)
    cos = jnp.tile(cos, (1, LANES // half))
    sin = jnp.tile(jnp.concatenate([-sin, sin], axis=1), (1, LANES // (2 * half)))
    return cos, sin


def _proj_weight(w_in):
    d = w_in.shape[0]
    z64 = jnp.zeros((d, 64), w_in.dtype)
    o = 0
    aq = w_in[:, o:o + 512]; o += 512
    ak = w_in[:, o:o + 128]; o += 128
    av = w_in[:, o:o + 128]; o += 128
    iq = w_in[:, o:o + 256]; o += 256
    ik = w_in[:, o:o + 64]; o += 64
    iw = w_in[:, o:o + 4]; o += 4
    bq = w_in[:, o:o + 512]; o += 512
    bk = w_in[:, o:o + 512]; o += 512
    bv = w_in[:, o:o + 512]; o += 512
    cols = []
    for h in range(A_HEADS):
        qh = aq[:, 64 * h:64 * h + 64]
        cols += [qh, z64] if h // (A_HEADS // A_KV_HEADS) == 0 else [z64, qh]
    cols += [ak, av, ik, iw, jnp.zeros((d, 60), w_in.dtype)]
    for h in range(IDX_HEADS):
        cols += [iq[:, 64 * h:64 * h + 64], z64]
    for j in range(2 * B_HEADS):
        qj = bq[:, 64 * j:64 * j + 64]
        cols += [qj, z64] if j % 2 == 0 else [z64, qj]
    cols += [bk, bv]
    w = jnp.concatenate(cols, axis=1)
    assert w.shape[1] == PROJ_WIDTH
    return w.astype(BF16)


def _proj_body(x_ref, g_ref, w_ref, cos_ref, sin_ref,
               qa_ref, ka32_ref, va32_ref, ik32_ref, ka16_ref, va16_ref, ik16_ref, qi_ref,
               bq_ref, bk32_ref, bv32_ref, bk16_ref, bv16_ref):
    h = _rms(x_ref[...], g_ref[...]).astype(BF16)
    cos, sin = cos_ref[...], sin_ref[...]
    lane = lax.broadcasted_iota(I32, cos.shape, 1)
    first_half = (lane & 63) < 32

    def rope(y):
        rot = jnp.where(first_half, pltpu.roll(y, LANES - 32, 1), pltpu.roll(y, 32, 1))
        return y * cos + rot * sin

    def seg(name):
        c0, width = PROJ_COLS[name]
        return _dot(h, w_ref[:, c0:c0 + width])

    def groups(y):
        return [y[:, LANES * j:LANES * (j + 1)] for j in range(y.shape[1] // LANES)]

    for j, y in enumerate(groups(seg("aq"))):
        qa_ref[:, LANES * j:LANES * (j + 1)] = (rope(y) * 0.125).astype(BF16)
    for j, y in enumerate(groups(seg("iq"))):
        qi_ref[:, LANES * j:LANES * (j + 1)] = (rope(y) * 0.125).astype(BF16)
    for j, y in enumerate(groups(seg("bq"))):
        bq_ref[:, LANES * j:LANES * (j + 1)] = (rope(y) * 0.125).astype(BF16)
    ka = rope(seg("ak"))
    ka32_ref[...] = ka
    ka16_ref[...] = ka.astype(BF16)
    va = seg("av")
    va32_ref[...] = va
    va16_ref[...] = va.astype(BF16)
    y = seg("ik")
    ik = jnp.where(lane < 64, rope(y), y)
    ik32_ref[...] = ik
    ik16_ref[...] = ik.astype(BF16)
    for j, y in enumerate(groups(seg("bk"))):
        r = rope(y)
        bk32_ref[:, LANES * j:LANES * (j + 1)] = r
        bk16_ref[:, LANES * j:LANES * (j + 1)] = r.astype(BF16)
    bv = seg("bv")
    bv32_ref[...] = bv
    bv16_ref[...] = bv.astype(BF16)


def _proj(x, g, w, cos, sin):
    rows, d = x.shape
    period = cos.shape[0]
    tm = _row_tile(period)
    n_t = period // tm
    row = lambda width: pl.BlockSpec((tm, width), lambda i: (i, 0))
    tab = pl.BlockSpec((tm, LANES), lambda i: (i % n_t, 0))
    names = ["qa", "ka32", "va32", "ik32", "ka16", "va16", "ik16", "qi", "bq", "bk32", "bv32", "bk16", "bv16"]
    widths = [1024, 128, 128, 128, 128, 128, 128, 512, 1024, 512, 512, 512, 512]
    dtypes = [BF16, F32, F32, F32, BF16, BF16, BF16, BF16, BF16, F32, F32, BF16, BF16]
    out = pl.pallas_call(
        _proj_body, grid=(rows // tm,),
        in_specs=[row(d), _resident((1, d)), _resident(w.shape), tab, tab],
        out_specs=[row(wd) for wd in widths],
        out_shape=[jax.ShapeDtypeStruct((rows, wd), dt) for wd, dt in zip(widths, dtypes)],
        compiler_params=_cparams(("arbitrary",)), name="proj",
    )(x, g.reshape(1, d), w, cos, sin)
    return dict(zip(names, out))


def _memkv_body(x_ref, g_ref, w_ref, k32_ref, v32_ref, k16_ref, v16_ref):
    h = _rms(x_ref[...], g_ref[...]).astype(BF16)
    y = _dot(h, w_ref[...])
    half = y.shape[1] // 2
    k, v = y[:, :half], y[:, half:]
    k32_ref[...] = k
    v32_ref[...] = v
    k16_ref[...] = k.astype(BF16)
    v16_ref[...] = v.astype(BF16)


def _memkv(x, g, w):
    rows, d = x.shape
    half = w.shape[1] // 2
    tm = _row_tile(rows)
    row = lambda width: pl.BlockSpec((tm, width), lambda i: (i, 0))
    return pl.pallas_call(
        _memkv_body, grid=(rows // tm,),
        in_specs=[row(d), _resident((1, d)), _resident(w.shape)],
        out_specs=[row(half)] * 4,
        out_shape=[jax.ShapeDtypeStruct((rows, half), dt) for dt in (F32, F32, BF16, BF16)],
        compiler_params=_cparams(("arbitrary",)), name="memkv",
    )(x, g.reshape(1, d), w)


def _float_key(s):
    bits = pltpu.bitcast(s + 0.0, I32)
    return bits ^ ((bits >> 31) & 0x7FFFFFFF)


def _kth_largest_key(count_ge, rows_shape, k):
    base = jnp.where(count_ge(jnp.zeros(rows_shape, I32)) >= k, 0, INT_MIN).astype(I32)

    def bit_step(i, base):
        cand = base | (jnp.int32(1) << (30 - i))
        return jnp.where(count_ge(cand) >= k, cand, base)

    return lax.fori_loop(0, 31, bit_step, base)


def _softmax_step(s, m_prev, l_prev):
    m_new = jnp.maximum(m_prev, jnp.max(s, axis=-1, keepdims=True))
    alpha = jnp.exp(m_prev - m_new)
    p = jnp.exp(s - m_new)
    l_new = alpha * l_prev + jnp.sum(p, axis=-1, keepdims=True)
    return p, m_new, l_new, alpha


def _dsa_body(qa_ref, qi_ref, ikq_ref, ka_ref, va_ref, ik_ref, tri_ref, o_ref,
              key_scr, m_scr, l_scr, acc_scr, *, tq, tk, top_k):
    q0 = pl.program_id(1) * tq
    n_kb = (q0 + tq + tk - 1) // tk
    rep = A_HEADS // A_KV_HEADS
    row_pos = q0 + lax.broadcasted_iota(I32, (tq, 1), 0)
    col_iota = lax.broadcasted_iota(I32, (1, tk), 1)

    qi = jnp.concatenate([qi_ref[:, LANES * h:LANES * (h + 1)] for h in range(IDX_HEADS)], axis=0)
    ikq = ikq_ref[...]
    w_heads = [ikq[:, 64 + h:65 + h] for h in range(IDX_HEADS)]

    def score_block(kb, carry):
        k0 = pl.multiple_of(kb * tk, tk)
        logits = _dot_nt(qi, ik_ref[pl.ds(k0, tk), :])
        s = jnp.zeros((tq, tk), F32)
        for h in range(IDX_HEADS):
            s = s + jnp.maximum(logits[h * tq:(h + 1) * tq], 0.0) * w_heads[h]
        s = jnp.where(k0 + col_iota <= row_pos, s, -jnp.inf)
        key_scr[kb] = _float_key(s)
        return carry

    lax.fori_loop(0, n_kb, score_block, 0)

    def count_ge(cand):
        def blk(kb, acc):
            c = jnp.where(key_scr[kb] >= cand, 1, 0)
            for j in range(tk // LANES):
                acc = acc + c[:, LANES * j:LANES * (j + 1)]
            return acc
        acc = lax.fori_loop(0, n_kb, blk, jnp.zeros((tq, LANES), I32))
        return jnp.sum(acc, axis=1, keepdims=True)

    thr = _kth_largest_key(count_ge, (tq, 1), top_k)
    need = (top_k - count_ge(thr + 1)).astype(F32)
    thr_tie = jnp.where(thr == KEY_NEG_INF, jnp.int32(2 ** 31 - 1), thr)

    q_groups = [jnp.concatenate([qa_ref[:, LANES * (g * rep + r):LANES * (g * rep + r + 1)]
                                 for r in range(rep)], axis=0) for g in range(A_KV_HEADS)]
    m_scr[...] = jnp.full(m_scr.shape, NEG_BIG, F32)
    l_scr[...] = jnp.zeros(l_scr.shape, F32)
    acc_scr[...] = jnp.zeros(acc_scr.shape, F32)

    def attend_block(kb, tie_seen):
        k0 = pl.multiple_of(kb * tk, tk)
        key = key_scr[kb]
        tie = key == thr_tie
        rank = tie_seen + _dot(jnp.where(tie, 1.0, 0.0).astype(BF16), tri_ref[...])
        bias = jnp.where(key > thr, 0.0, jnp.where(tie, jnp.where(rank <= need, 0.0, NEG_BIG), NEG_BIG))
        kblk = ka_ref[pl.ds(k0, tk), :]
        vblk = va_ref[pl.ds(k0, tk), :]
        for g in range(A_KV_HEADS):
            s = _dot_nt(q_groups[g], kblk)
            s = jnp.concatenate([s[r * tq:(r + 1) * tq] + bias for r in range(rep)], axis=0)
            p, m_new, l_new, alpha = _softmax_step(s, m_scr[g], l_scr[g])
            m_scr[g] = m_new
            l_scr[g] = l_new
            acc_scr[g] = alpha * acc_scr[g] + _dot(p.astype(BF16), vblk)
        return rank[:, tk - 1:tk]

    lax.fori_loop(0, n_kb, attend_block, jnp.zeros((tq, 1), F32))

    lane = lax.broadcasted_iota(I32, (tq, LANES), 1)
    for g in range(A_KV_HEADS):
        o = acc_scr[g] / l_scr[g]
        for j in range(rep // 2):
            a, b = o[(2 * j) * tq:(2 * j + 1) * tq], o[(2 * j + 1) * tq:(2 * j + 2) * tq]
            if g == 0:
                pair = jnp.where(lane < 64, a, pltpu.roll(b, 64, 1))
            else:
                pair = jnp.where(lane < 64, pltpu.roll(a, 64, 1), b)
            c = g * (rep // 2) + j
            o_ref[:, LANES * c:LANES * (c + 1)] = pair.astype(BF16)


def _dsa_prompt(p, n_batch, seq, top_k):
    tq = min(128, seq)
    tk = min(512, seq)
    rep = A_HEADS // A_KV_HEADS
    n_q = seq // tq
    tri = jnp.triu(jnp.ones((tk, tk), BF16))
    qrow = lambda width: pl.BlockSpec((tq, width), lambda b, i: (b * n_q + i, 0))
    seqblk = lambda width: pl.BlockSpec((seq, width), lambda b, i: (b, 0))
    return pl.pallas_call(
        functools.partial(_dsa_body, tq=tq, tk=tk, top_k=top_k),
        grid=(n_batch, n_q),
        in_specs=[qrow(1024), qrow(512), qrow(LANES), seqblk(LANES), seqblk(LANES), seqblk(LANES),
                  _resident((tk, tk))],
        out_specs=qrow(512),
        out_shape=jax.ShapeDtypeStruct((n_batch * seq, 512), BF16),
        scratch_shapes=[pltpu.VMEM((seq // tk, tq, tk), I32),
                        pltpu.VMEM((A_KV_HEADS, rep * tq, 1), F32),
                        pltpu.VMEM((A_KV_HEADS, rep * tq, 1), F32),
                        pltpu.VMEM((A_KV_HEADS, rep * tq, LANES), F32)],
        compiler_params=_cparams(("arbitrary", "arbitrary")), name="dsa_prompt",
    )(p["qa"], p["qi"], p["ik32"], p["ka16"], p["va16"], p["ik16"], tri)


def _diff_lambda(lq1, lk1, lq2, lk2, lambda_init):
    s1 = jnp.sum(lq1 * lk1, axis=-1, keepdims=True)
    s2 = jnp.sum(lq2 * lk2, axis=-1, keepdims=True)
    return jnp.exp(s1) - jnp.exp(s2) + lambda_init


def _subln(o, gain, lambda_init):
    return _rms(o, gain) * (1.0 - lambda_init)


def _diff_body(bq_ref, bk_ref, bv_ref, lq1_ref, lk1_ref, lq2_ref, lk2_ref, gain_ref, o_ref,
               m_scr, l_scr, acc_scr, *, tq, tk, lambda_init):
    qi = pl.program_id(1)
    q0 = qi * tq
    n_diag = tq // tk
    n_full = qi * n_diag
    rp = q0 + lax.broadcasted_iota(I32, (tq, 1), 0)
    row_pos = jnp.concatenate([rp, rp], axis=0)
    col_iota = lax.broadcasted_iota(I32, (1, tk), 1)
    m_scr[...] = jnp.full(m_scr.shape, NEG_BIG, F32)
    l_scr[...] = jnp.zeros(l_scr.shape, F32)
    acc_scr[...] = jnp.zeros(acc_scr.shape, F32)
    q_heads = [jnp.concatenate([bq_ref[:, LANES * (2 * h + c):LANES * (2 * h + c + 1)] for c in range(2)], axis=0)
               for h in range(B_HEADS)]

    def block(kb, masked):
        k0 = pl.multiple_of(kb * tk, tk)
        for h in range(B_HEADS):
            s = _dot_nt(q_heads[h], bk_ref[pl.ds(k0, tk), LANES * h:LANES * (h + 1)])
            if masked:
                s = jnp.where(k0 + col_iota <= row_pos, s, NEG_BIG)
            p, m_new, l_new, alpha = _softmax_step(s, m_scr[h], l_scr[h])
            m_scr[h] = m_new
            l_scr[h] = l_new
            acc_scr[h] = alpha * acc_scr[h] + _dot(p.astype(BF16), bv_ref[pl.ds(k0, tk), LANES * h:LANES * (h + 1)])

    def full_block(kb, carry):
        block(kb, False)
        return carry

    lax.fori_loop(0, n_full, full_block, 0)
    for d in range(n_diag):
        block(n_full + d, True)

    lam = _diff_lambda(lq1_ref[...], lk1_ref[...], lq2_ref[...], lk2_ref[...], lambda_init)
    for h in range(B_HEADS):
        o = acc_scr[h] / l_scr[h]
        o = o[:tq] - lam * o[tq:]
        o_ref[:, LANES * h:LANES * (h + 1)] = _subln(o, gain_ref[...], lambda_init).astype(BF16)


def _diff_prompt(p, lam_params, gain, n_batch, seq, lambda_init):
    tq = min(256, seq)
    tk = tq
    n_q = seq // tq
    qrow = lambda width: pl.BlockSpec((tq, width), lambda b, i: (b * n_q + i, 0))
    seqblk = lambda width: pl.BlockSpec((seq, width), lambda b, i: (b, 0))
    small = [_resident((1, a.shape[-1])) for a in lam_params] + [_resident((1, B_V_DIM))]
    return pl.pallas_call(
        functools.partial(_diff_body, tq=tq, tk=tk, lambda_init=lambda_init),
        grid=(n_batch, n_q),
        in_specs=[qrow(1024), seqblk(512), seqblk(512)] + small,
        out_specs=qrow(512),
        out_shape=jax.ShapeDtypeStruct((n_batch * seq, 512), BF16),
        scratch_shapes=[pltpu.VMEM((B_HEADS, 2 * tq, 1), F32),
                        pltpu.VMEM((B_HEADS, 2 * tq, 1), F32),
                        pltpu.VMEM((B_HEADS, 2 * tq, LANES), F32)],
        compiler_params=_cparams(("arbitrary", "arbitrary")), name="diff_prompt",
    )(p["bq"], p["bk16"], p["bv16"], *[a.reshape(1, -1) for a in lam_params], gain.reshape(1, -1))


def _post_body(*refs, mem_in_kernel):
    if mem_in_kernel:
        (x_ref, g_ref, oa_ref, ob_ref, mk_ref, mv_ref, wmq_ref, wg_ref, wpa_ref, wpb_ref, wpm_ref, wo_ref,
         o_ref) = refs
    else:
        x_ref, g_ref, oa_ref, ob_ref, om_ref, wg_ref, wpa_ref, wpb_ref, wpm_ref, wo_ref, o_ref = refs
    x = x_ref[...]
    d = x.shape[1]
    h = _rms(x, g_ref[...]).astype(BF16)
    if mem_in_kernel:
        mq = (_dot(h, wmq_ref[...]) * (M_HEAD_DIM ** -0.5)).astype(BF16)
        heads = []
        for hd in range(M_HEADS):
            sl = slice(LANES * hd, LANES * (hd + 1))
            s = _dot_nt(mq[:, sl], mk_ref[:, sl])
            e = jnp.exp(s - jnp.max(s, axis=-1, keepdims=True))
            o = _dot(e.astype(BF16), mv_ref[:, sl]) / jnp.sum(e, axis=-1, keepdims=True)
            heads.append(o.astype(BF16))
        om = jnp.concatenate(heads, axis=1)
    else:
        om = om_ref[...]

    def gate(i):
        return jax.nn.sigmoid(_dot(h, wg_ref[:, d * i:d * (i + 1)]))

    merged = gate(0) * _dot(oa_ref[...], wpa_ref[...])
    merged = merged + gate(1) * _dot(ob_ref[...], wpb_ref[...])
    merged = merged + gate(2) * _dot(om, wpm_ref[...])
    o_ref[...] = x + _dot(merged.astype(BF16), wo_ref[...])


def _post(x, g, oa, ob, w, *, mem=None, om=None, n_batch=1):
    rows, d = x.shape
    seq = rows // n_batch
    tm = _row_tile(seq)
    n_t = seq // tm
    row = lambda width: pl.BlockSpec((tm, width), lambda b, i: (b * n_t + i, 0))
    weights = [w["gates"], w["proj_a"], w["proj_b"], w["proj_m"], w["out"]]
    if mem is not None:
        mk, mv, w_mq = mem
        memblk = pl.BlockSpec((MEM_TOKENS, mk.shape[1]), lambda b, i: (b, 0))
        args = [x, g.reshape(1, d), oa, ob, mk, mv, w_mq] + weights
        in_specs = ([row(d), _resident((1, d)), row(512), row(512), memblk, memblk, _resident(w_mq.shape)]
                    + [_resident(a.shape) for a in weights])
    else:
        args = [x, g.reshape(1, d), oa, ob, om] + weights
        in_specs = [row(d), _resident((1, d)), row(512), row(512), row(512)] + [_resident(a.shape) for a in weights]
    return pl.pallas_call(
        functools.partial(_post_body, mem_in_kernel=mem is not None),
        grid=(n_batch, n_t), in_specs=in_specs, out_specs=row(d),
        out_shape=jax.ShapeDtypeStruct((rows, d), F32),
        compiler_params=_cparams(("arbitrary", "arbitrary")), name="post",
    )(*args)


def _mem_sample_body(x_ref, g_ref, wmq_ref, mk_ref, mv_ref, o_ref, mq_scr):
    b = pl.program_id(0)

    @pl.when(b == 0)
    def _():
        h = _rms(x_ref[...], g_ref[...]).astype(BF16)
        mq_scr[...] = _dot(h, wmq_ref[...]) * (M_HEAD_DIM ** -0.5)

    q = mq_scr[pl.ds(b, 1), :]
    for hd in range(M_HEADS):
        sl = slice(LANES * hd, LANES * (hd + 1))
        rows = pl.ds(hd, MEM_TOKENS, stride=M_HEADS)
        s = jnp.sum(mk_ref[rows, :] * q[:, sl], axis=-1, keepdims=True)
        e = jnp.exp(s - jnp.max(s, axis=0, keepdims=True))
        o = jnp.sum(e * mv_ref[rows, :], axis=0, keepdims=True) / jnp.sum(e, axis=0, keepdims=True)
        o_ref[:, sl] = o.astype(BF16)


def _mem_sample(x, g, w_mq, cache_k, cache_v, layer):
    n, d = x.shape
    width = w_mq.shape[1]
    memblk = pl.BlockSpec((None, None) + cache_k.shape[2:], lambda b: (b, layer, 0, 0))
    return pl.pallas_call(
        _mem_sample_body, grid=(n,),
        in_specs=[_resident((n, d)), _resident((1, d)), _resident(w_mq.shape), memblk, memblk],
        out_specs=pl.BlockSpec((None, 1, width), lambda b: (b, 0, 0)),
        out_shape=jax.ShapeDtypeStruct((n, 1, width), BF16),
        scratch_shapes=[pltpu.VMEM((n, width), F32)],
        compiler_params=_cparams(("arbitrary",)), name="mem_sample",
    )(x, g.reshape(1, d), w_mq, cache_k, cache_v)


def _page_copies(pt_ref, b, chunk, pages_per_chunk, layer, streams):
    slot = chunk % 2
    out = []
    for cache, dst, sem in streams:
        for j in range(pages_per_chunk):
            page = pt_ref[b, chunk * pages_per_chunk + j]
            out.append(pltpu.make_async_copy(cache.at[page, layer], dst(slot, j), sem.at[slot]))
    return out


def _lane_pages(buf, page):
    return lambda slot, j: buf.at[slot, :, pl.ds(page * j, page)]


def _row_pages(buf, rows):
    return lambda slot, j: buf.at[slot, pl.ds(rows * j, rows), :]


def _chunks(n_pages):
    ppc = min(16, n_pages)
    assert n_pages % ppc == 0
    return ppc, n_pages // ppc


def _idx_sample_body(pt_ref, qi_ref, w_ref, ikn_ref, cache_ref, s_ref, buf, sem, *, layer, n_pages, page):
    b = pl.program_id(0)
    ppc, n_ch = _chunks(n_pages)
    ck = ppc * page
    streams = [(cache_ref, _lane_pages(buf, page), sem)]
    for cp in _page_copies(pt_ref, b, 0, ppc, layer, streams):
        cp.start()
    q = qi_ref[:, 0:IDX_DIM]
    w = w_ref[...]

    def scores(logits):
        return jnp.sum(jnp.maximum(logits, 0.0) * w, axis=0, keepdims=True)

    for c in range(n_ch):
        if c + 1 < n_ch:
            for cp in _page_copies(pt_ref, b, c + 1, ppc, layer, streams):
                cp.start()
        for cp in _page_copies(pt_ref, b, c, ppc, layer, streams):
            cp.wait()
        s_ref[:, ck * c:ck * (c + 1)] = scores(_dot(q, buf[c % 2].astype(BF16)))
    lane = lax.broadcasted_iota(I32, (1, LANES), 1)
    new_key = ikn_ref[:, 0:IDX_DIM].astype(BF16).astype(F32)
    s_new = scores(jnp.sum(q.astype(F32) * new_key, axis=1, keepdims=True))
    s_ref[:, ck * n_ch:] = jnp.where(lane == 0, s_new, -jnp.inf)


def _idx_sample(page_table, qi8, w8, ik_new, cache_t, layer):
    n, n_pages = page_table.shape
    feat, page = cache_t.shape[2], cache_t.shape[3]
    ppc, n_ch = _chunks(n_pages)
    nk = n_pages * page + LANES
    grid_spec = pltpu.PrefetchScalarGridSpec(
        num_scalar_prefetch=1, grid=(n,),
        in_specs=[pl.BlockSpec((None, 8, LANES), lambda b, pt: (b, 0, 0)),
                  pl.BlockSpec((None, 8, 1), lambda b, pt: (b, 0, 0)),
                  pl.BlockSpec((None, 1, LANES), lambda b, pt: (b, 0, 0)),
                  pl.BlockSpec(memory_space=pl.ANY)],
        out_specs=pl.BlockSpec((None, 1, nk), lambda b, pt: (b, 0, 0)),
        scratch_shapes=[pltpu.VMEM((2, feat, ppc * page), F32), pltpu.SemaphoreType.DMA((2,))])
    return pl.pallas_call(
        functools.partial(_idx_sample_body, layer=layer, n_pages=n_pages, page=page),
        grid_spec=grid_spec, out_shape=jax.ShapeDtypeStruct((n, 1, nk), F32),
        compiler_params=_cparams(("arbitrary",)), name="idx_sample",
    )(page_table, qi8, w8, ik_new, cache_t)


def _select_body(s_ref, bias_ref, *, top_k, n_valid):
    key = _float_key(s_ref[...])
    rows, nk = key.shape
    shape = (rows, 1)

    def count(ones):
        return jnp.sum(ones, axis=1, keepdims=True)

    thr = _kth_largest_key(lambda cand: count(jnp.where(key >= cand, 1, 0)), shape, top_k)
    need = top_k - count(jnp.where(key > thr, 1, 0))
    thr_tie = jnp.where(thr == KEY_NEG_INF, jnp.int32(2 ** 31 - 1), thr)
    tie = jnp.where(key == thr_tie, 1, 0)
    idx = lax.broadcasted_iota(I32, key.shape, 1)
    n_bits = max(1, (nk - 1).bit_length())

    def bit_step(i, last):
        cand = last - (jnp.int32(1) << (n_bits - 1 - i))
        return jnp.where(count(jnp.where(idx <= cand, tie, 0)) >= need, cand, last)

    last = lax.fori_loop(0, n_bits, bit_step, jnp.full(shape, 2 ** n_bits - 1, I32))
    tie_bias = jnp.where(idx <= last, jnp.where(tie > 0, 0.0, NEG_BIG), NEG_BIG)
    bias = jnp.where(key > thr, 0.0, tie_bias)
    bias_ref[...] = jnp.where(idx < n_valid, bias, NEG_BIG)


def _select(scores, top_k, n_valid):
    return pl.pallas_call(
        functools.partial(_select_body, top_k=top_k, n_valid=n_valid),
        out_shape=jax.ShapeDtypeStruct(scores.shape, F32),
        compiler_params=pltpu.CompilerParams(vmem_limit_bytes=V7X_VMEM_LIMIT_BYTES), name="select_sample",
    )(scores)


def _bf16_round(x):
    return x.astype(BF16).astype(F32)


def _new_row_step(s_new, v_new, m, l, acc):
    m_new = jnp.maximum(m, s_new)
    alpha = jnp.exp(m - m_new)
    p_new = jnp.exp(s_new - m_new)
    return alpha * l + p_new, alpha * acc + p_new * v_new


def _dsa_sample_body(pt_ref, q_ref, kn_ref, vn_ref, bias_ref, kc_ref, vc_ref, o_ref, kbuf, vbuf, sem,
                     *, layer, n_pages, page):
    b = pl.program_id(0)
    ppc, n_ch = _chunks(n_pages)
    ck = ppc * page
    streams = [(kc_ref, _lane_pages(kbuf, page), sem.at[0]), (vc_ref, _lane_pages(vbuf, page), sem.at[1])]
    for cp in _page_copies(pt_ref, b, 0, ppc, layer, streams):
        cp.start()
    q = q_ref[...]
    m = jnp.full((8, 1), NEG_BIG, F32)
    l = jnp.zeros((8, 1), F32)
    acc = jnp.zeros((8, LANES), F32)
    for c in range(n_ch):
        if c + 1 < n_ch:
            for cp in _page_copies(pt_ref, b, c + 1, ppc, layer, streams):
                cp.start()
        for cp in _page_copies(pt_ref, b, c, ppc, layer, streams):
            cp.wait()
        s = _dot(q, kbuf[c % 2].astype(BF16)) + bias_ref[:, ck * c:ck * (c + 1)]
        p, m, l, alpha = _softmax_step(s, m, l)
        acc = alpha * acc + _dot_nt(p.astype(BF16), vbuf[c % 2].astype(BF16))
    s_new = (jnp.sum(q.astype(F32) * _bf16_round(kn_ref[...]), axis=1, keepdims=True)
             + bias_ref[:, ck * n_ch:ck * n_ch + 1])
    l, acc = _new_row_step(s_new, _bf16_round(vn_ref[...]), m, l, acc)
    o_ref[...] = acc / l


def _diff_sample_body(pt_ref, q_ref, kn_ref, vn_ref, kc_ref, vc_ref, lq1_ref, lk1_ref, lq2_ref, lk2_ref,
                      gain_ref, o_ref, kbuf, vbuf, sem, *, layer, n_pages, page, lambda_init):
    b = pl.program_id(0)
    ppc, n_ch = _chunks(n_pages)
    ck = ppc * page
    page_rows = page * B_HEADS
    streams = [(kc_ref, _row_pages(kbuf, page_rows), sem.at[0]), (vc_ref, _row_pages(vbuf, page_rows), sem.at[1])]
    for cp in _page_copies(pt_ref, b, 0, ppc, layer, streams):
        cp.start()
    q = q_ref[...]
    row_head = lax.broadcasted_iota(I32, (8, 1), 0) >> 1
    m = jnp.full((8, 1), NEG_BIG, F32)
    l = jnp.zeros((8, 1), F32)
    acc = jnp.zeros((8, B_HEADS * LANES), F32)
    for c in range(n_ch):
        if c + 1 < n_ch:
            for cp in _page_copies(pt_ref, b, c + 1, ppc, layer, streams):
                cp.start()
        for cp in _page_copies(pt_ref, b, c, ppc, layer, streams):
            cp.wait()
        head_rows = lambda buf, h: buf[c % 2, pl.ds(h, ck, stride=B_HEADS), :].astype(BF16)
        s = jnp.zeros((8, ck), F32)
        for h in range(B_HEADS):
            s = jnp.where(row_head == h, _dot_nt(q, head_rows(kbuf, h)), s)
        p, m, l, alpha = _softmax_step(s, m, l)
        pb = p.astype(BF16)
        acc = alpha * acc + jnp.concatenate([_dot(pb, head_rows(vbuf, h)) for h in range(B_HEADS)], axis=1)
    qf = q.astype(F32)
    kn = _bf16_round(kn_ref[...])
    s_new = jnp.zeros((8, 1), F32)
    for h in range(B_HEADS):
        s_h = jnp.sum(qf * kn[:, LANES * h:LANES * (h + 1)], axis=1, keepdims=True)
        s_new = jnp.where(row_head == h, s_h, s_new)
    l, acc = _new_row_step(s_new, _bf16_round(vn_ref[...]), m, l, acc)
    o = acc / l
    lam = _diff_lambda(lq1_ref[...], lk1_ref[...], lq2_ref[...], lk2_ref[...], lambda_init)
    for h in range(B_HEADS):
        sl = slice(LANES * h, LANES * (h + 1))
        oh = o[2 * h:2 * h + 1, sl] - lam * o[2 * h + 1:2 * h + 2, sl]
        o_ref[:, sl] = _subln(oh, gain_ref[...], lambda_init).astype(BF16)


def _per_sample(a):
    return pl.BlockSpec((None,) + a.shape[1:], lambda b, pt: (b,) + (0,) * (a.ndim - 1))


def _dsa_sample(page_table, q8, k_new, v_new, bias, cache_kt, cache_vt, layer):
    n, n_pages = page_table.shape
    feat, page = cache_kt.shape[2], cache_kt.shape[3]
    ppc, _ = _chunks(n_pages)
    any_spec = pl.BlockSpec(memory_space=pl.ANY)
    grid_spec = pltpu.PrefetchScalarGridSpec(
        num_scalar_prefetch=1, grid=(n,),
        in_specs=[_per_sample(q8), _per_sample(k_new), _per_sample(v_new), _per_sample(bias), any_spec, any_spec],
        out_specs=pl.BlockSpec((None, 8, feat), lambda b, pt: (b, 0, 0)),
        scratch_shapes=[pltpu.VMEM((2, feat, ppc * page), F32), pltpu.VMEM((2, feat, ppc * page), F32),
                        pltpu.SemaphoreType.DMA((2, 2))])
    return pl.pallas_call(
        functools.partial(_dsa_sample_body, layer=layer, n_pages=n_pages, page=page),
        grid_spec=grid_spec, out_shape=jax.ShapeDtypeStruct((n, 8, feat), F32),
        compiler_params=_cparams(("arbitrary",)), name="dsa_sample",
    )(page_table, q8, k_new, v_new, bias, cache_kt, cache_vt)


def _diff_sample(page_table, q8, k_new, v_new, cache_k, cache_v, layer, diff_params, lambda_init):
    n, n_pages = page_table.shape
    page_rows = cache_k.shape[2]
    ppc, _ = _chunks(n_pages)
    width = B_HEADS * B_V_DIM
    any_spec = pl.BlockSpec(memory_space=pl.ANY)
    extra = [a.reshape(1, -1) for a in diff_params]
    grid_spec = pltpu.PrefetchScalarGridSpec(
        num_scalar_prefetch=1, grid=(n,),
        in_specs=([_per_sample(q8), _per_sample(k_new), _per_sample(v_new), any_spec, any_spec]
                  + [pl.BlockSpec(a.shape, lambda b, pt: (0, 0)) for a in extra]),
        out_specs=pl.BlockSpec((None, 1, width), lambda b, pt: (b, 0, 0)),
        scratch_shapes=[pltpu.VMEM((2, ppc * page_rows, LANES), F32), pltpu.VMEM((2, ppc * page_rows, LANES), F32),
                        pltpu.SemaphoreType.DMA((2, 2))])
    return pl.pallas_call(
        functools.partial(_diff_sample_body, layer=layer, n_pages=n_pages, page=page_rows // B_HEADS,
                          lambda_init=lambda_init),
        grid_spec=grid_spec, out_shape=jax.ShapeDtypeStruct((n, 1, width), BF16),
        compiler_params=_cparams(("arbitrary",)), name="diff_sample",
    )(page_table, q8, k_new, v_new, cache_k, cache_v, *extra)


def _layer_weights(l, w_in, w_proj_a, w_proj_b, w_proj_m, w_out):
    o_mq = sum((512, 128, 128, 256, 64, 4, 512, 512, 512))
    return dict(proj=_proj_weight(w_in[l]),
                mq=w_in[l][:, o_mq:o_mq + 512].astype(BF16),
                gates=w_in[l][:, o_mq + 512:].astype(BF16),
                proj_a=w_proj_a[l].astype(BF16), proj_b=w_proj_b[l].astype(BF16),
                proj_m=w_proj_m[l].astype(BF16), out=w_out[l].astype(BF16))


def kernel(x_prompt, x_sample, cache_a_k, cache_a_v, cache_idx_k, cache_b_k, cache_b_v, cache_mem_k, cache_mem_v, page_table, mem_prompt, norm_ffn1, ffn1_w_in, ffn1_w_out, norm_mix, w_in, lambda_q1, lambda_k1, lambda_q2, lambda_k2, subln_gain, norm_mem, w_mem_kv, w_proj_a, w_proj_b, w_proj_m, w_out, norm_ffn2, ffn2_w_in, ffn2_w_out, final_norm):
    n_b, seq, d = x_prompt.shape
    n_s, dec_seq, _ = x_sample.shape
    assert dec_seq == 1
    depth = norm_ffn1.shape[0]
    n_pool, _, page = cache_a_k.shape[:3]
    n_pages = page_table.shape[1]
    past_len = n_pages * page
    top_k_prompt = min(IDX_TOPK_MAX, seq // 4)
    top_k_sample = min(IDX_TOPK_MAX, (past_len + dec_seq) // 4)

    cos_p, sin_p = _rope_tables(jnp.arange(seq, dtype=I32))
    cos_s, sin_s = _rope_tables(jnp.full((n_s,), past_len, I32))
    c_ik = jnp.transpose(cache_idx_k, (0, 1, 3, 2))
    c_ak = jnp.transpose(cache_a_k, (0, 1, 3, 4, 2)).reshape(n_pool, depth, -1, page)
    c_av = jnp.transpose(cache_a_v, (0, 1, 3, 4, 2)).reshape(n_pool, depth, -1, page)
    c_bk = cache_b_k.reshape(n_pool, depth, page * B_HEADS, -1)
    c_bv = cache_b_v.reshape(n_pool, depth, page * B_HEADS, -1)
    c_mk = cache_mem_k.reshape(n_s, depth, MEM_TOKENS * M_HEADS, -1)
    c_mv = cache_mem_v.reshape(n_s, depth, MEM_TOKENS * M_HEADS, -1)
    mem_rows = mem_prompt.reshape(n_b * MEM_TOKENS, d)

    xp = x_prompt.reshape(n_b * seq, d)
    xs = x_sample.reshape(n_s, d)
    rows_p, rows_s, mem_rows_out = [], [], []
    yp = ys = None
    for l in range(depth):
        last = l == depth - 1
        lambda_init = 0.8 - 0.6 * math.exp(-0.3 * l)
        w = _layer_weights(l, w_in, w_proj_a, w_proj_b, w_proj_m, w_out)
        f1_in, f1_out = ffn1_w_in[l].astype(BF16), ffn1_w_out[l].astype(BF16)
        f2_in, f2_out = ffn2_w_in[l].astype(BF16), ffn2_w_out[l].astype(BF16)
        lam_params = (lambda_q1[l], lambda_k1[l], lambda_q2[l], lambda_k2[l])

        mk32, mv32, mk16, mv16 = _memkv(mem_rows, norm_mem[l], w_mem_kv[l].astype(BF16))
        mem_rows_out.append((mk32, mv32))
        xp = _ffn(xp, norm_ffn1[l], f1_in, f1_out)
        p = _proj(xp, norm_mix[l], w["proj"], cos_p, sin_p)
        rows_p.append(p)
        oa = _dsa_prompt(p, n_b, seq, top_k_prompt)
        ob = _diff_prompt(p, lam_params, subln_gain[l], n_b, seq, lambda_init)
        xp = _post(xp, norm_mix[l], oa, ob, w, mem=(mk16, mv16, w["mq"]), n_batch=n_b)
        if last:
            xp, yp = _ffn(xp, norm_ffn2[l], f2_in, f2_out, final_norm)
        else:
            xp = _ffn(xp, norm_ffn2[l], f2_in, f2_out)

        xs = _ffn(xs, norm_ffn1[l], f1_in, f1_out)
        s = _proj(xs, norm_mix[l], w["proj"], cos_s, sin_s)
        rows_s.append(s)
        qi8 = jnp.pad(s["qi"].reshape(n_s, IDX_HEADS, LANES), ((0, 0), (0, 8 - IDX_HEADS), (0, 0)))
        w8 = jnp.pad(s["ik32"][:, IDX_DIM:IDX_DIM + IDX_HEADS], ((0, 0), (0, 8 - IDX_HEADS)))[:, :, None]
        scores = _idx_sample(page_table, qi8, w8, s["ik32"].reshape(n_s, 1, LANES), c_ik, l)
        bias = _select(scores.reshape(n_s, -1), top_k_sample, past_len + 1).reshape(n_s, 1, -1)
        oa8 = _dsa_sample(page_table, s["qa"].reshape(n_s, A_HEADS, LANES), s["ka32"].reshape(n_s, 1, -1),
                          s["va32"].reshape(n_s, 1, -1), bias, c_ak, c_av, l)
        rep = A_HEADS // A_KV_HEADS
        oa_s = jnp.concatenate([oa8[:, h, 64 * (h // rep):64 * (h // rep) + 64] for h in range(A_HEADS)],
                               axis=1).astype(BF16)
        ob_s = _diff_sample(page_table, s["bq"].reshape(n_s, 2 * B_HEADS, LANES), s["bk32"].reshape(n_s, 1, -1),
                            s["bv32"].reshape(n_s, 1, -1), c_bk, c_bv, l, lam_params + (subln_gain[l],),
                            lambda_init).reshape(n_s, -1)
        om_s = _mem_sample(xs, norm_mix[l], w["mq"], c_mk, c_mv, l).reshape(n_s, -1)
        xs = _post(xs, norm_mix[l], oa_s, ob_s, w, om=om_s)
        if last:
            xs, ys = _ffn(xs, norm_ffn2[l], f2_in, f2_out, final_norm)
        else:
            xs = _ffn(xs, norm_ffn2[l], f2_in, f2_out)

    def stack(rows, name, n, t, shape):
        width = int(math.prod(shape))
        return jnp.stack([r[name][:, :width].reshape(n, t, *shape) for r in rows], axis=1)

    def stack_mem(i):
        return jnp.stack([r[i].reshape(n_b, MEM_TOKENS, M_HEADS, M_HEAD_DIM) for r in mem_rows_out], axis=1)

    kv = lambda rows, n, t: (
        stack(rows, "ka32", n, t, (A_KV_HEADS, A_HEAD_DIM)), stack(rows, "va32", n, t, (A_KV_HEADS, A_HEAD_DIM)),
        stack(rows, "ik32", n, t, (IDX_DIM,)), stack(rows, "bk32", n, t, (B_HEADS, 2 * B_QK_DIM)),
        stack(rows, "bv32", n, t, (B_HEADS, B_V_DIM)))
    return ((yp.reshape(n_b, seq, d), ys.reshape(n_s, dec_seq, d)) + kv(rows_p, n_b, seq)
            + (stack_mem(0), stack_mem(1)) + kv(rows_s, n_s, dec_seq))
```

```python
import functools
import math

import jax
import jax.numpy as jnp
from jax import lax
from jax.experimental import pallas as pl
from jax.experimental.pallas import tpu as pltpu

F32 = jnp.float32
BF16 = jnp.bfloat16
I32 = jnp.int32

A_HEADS, A_KV_HEADS, A_HEAD_DIM = 8, 2, 64
IDX_HEADS, IDX_DIM, IDX_TOPK_MAX = 4, 64, 256
B_HEADS, B_QK_DIM, B_V_DIM = 4, 64, 128
MEM_TOKENS, M_HEADS, M_HEAD_DIM = 256, 4, 128
N_BRANCH = 3
ROPE_THETA = 10000.0
RMS_EPS = 1e-6

LANES = 128
V7X_VMEM_LIMIT_BYTES = 60000 * 1024

NEG_BIG = -1e30
INT_MIN = -(2 ** 31)
KEY_NEG_INF = -(2 ** 31) + 0x7FFFFF

PROJ_COLS = dict(aq=(0, 1024), ak=(1024, 128), av=(1152, 128), ik=(1280, 128), iq=(1408, 512),
                 bq=(1920, 1024), bk=(2944, 512), bv=(3456, 512))
PROJ_WIDTH = 3968


def _cparams(sem):
    return pltpu.CompilerParams(dimension_semantics=sem, vmem_limit_bytes=V7X_VMEM_LIMIT_BYTES)


def _resident(shape):
    zeros = (0,) * len(shape)
    return pl.BlockSpec(shape, lambda *_: zeros, pipeline_mode=pl.Buffered(1))


def _rms(x, g):
    return x * lax.rsqrt(jnp.mean(x * x, axis=-1, keepdims=True) + RMS_EPS) * g


def _dot(a, b):
    return jnp.dot(a, b, preferred_element_type=F32)


def _dot_nt(a, b):
    return lax.dot_general(a, b, (((1,), (1,)), ((), ())), preferred_element_type=F32)


def _row_tile(rows, target=512):
    t = min(rows, target)
    assert rows % t == 0 and t % 8 == 0, (rows, t)
    return t


def _ffn_body(*refs, d_ff, chunk, final):
    if final:
        x_ref, g_ref, wi_ref, wo_ref, fg_ref, o_ref, of_ref = refs
    else:
        x_ref, g_ref, wi_ref, wo_ref, o_ref = refs
    x = x_ref[...]
    h = _rms(x, g_ref[...]).astype(BF16)
    acc = jnp.zeros_like(x)
    for c in range(d_ff // chunk):
        gate = _dot(h, wi_ref[:, c * chunk:(c + 1) * chunk])
        up = _dot(h, wi_ref[:, d_ff + c * chunk:d_ff + (c + 1) * chunk])
        act = (gate * jax.nn.sigmoid(gate) * up).astype(BF16)
        acc = acc + _dot(act, wo_ref[c * chunk:(c + 1) * chunk, :])
    y = x + 0.5 * acc
    o_ref[...] = y
    if final:
        of_ref[...] = _rms(y, fg_ref[...])


def _ffn(x, g, w_in, w_out, final_g=None):
    rows, d = x.shape
    d_ff = w_out.shape[0]
    chunk = d_ff // 2 if (d_ff // 2) % LANES == 0 else d_ff
    tm = _row_tile(rows)
    final = final_g is not None
    row_spec = pl.BlockSpec((tm, d), lambda i: (i, 0))
    in_specs = [row_spec, _resident((1, d)), _resident(w_in.shape), _resident(w_out.shape)]
    args = [x, g.reshape(1, d), w_in, w_out]
    out_shape = [jax.ShapeDtypeStruct((rows, d), F32)]
    out_specs = [row_spec]
    if final:
        in_specs.append(_resident((1, d)))
        args.append(final_g.reshape(1, d))
        out_shape.append(jax.ShapeDtypeStruct((rows, d), F32))
        out_specs.append(row_spec)
    out = pl.pallas_call(
        functools.partial(_ffn_body, d_ff=d_ff, chunk=chunk, final=final),
        grid=(rows // tm,), in_specs=in_specs, out_specs=out_specs, out_shape=out_shape,
        compiler_params=_cparams(("arbitrary",)), name="ffn_final" if final else "ffn",
    )(*args)
    return out if final else out[0]


def _rope_tables(pos):
    half = A_HEAD_DIM // 2
    inv = ROPE_THETA ** (-jnp.arange(half, dtype=F32) / half)
    ang = pos.astype(F32)[:, None] * inv[None, :]
    cos, sin = jnp.cos(ang), jnp.sin(ang)
    cos = jnp.tile(cos, (1, LANES // half))
    sin = jnp.tile(jnp.concatenate([-sin, sin], axis=1), (1, LANES // (2 * half)))
    return cos, sin


def _proj_weight(w_in):
    d = w_in.shape[0]
    z64 = jnp.zeros((d, 64), w_in.dtype)
    o = 0
    aq = w_in[:, o:o + 512]; o += 512
    ak = w_in[:, o:o + 128]; o += 128
    av = w_in[:, o:o + 128]; o += 128
    iq = w_in[:, o:o + 256]; o += 256
    ik = w_in[:, o:o + 64]; o += 64
    iw = w_in[:, o:o + 4]; o += 4
    bq = w_in[:, o:o + 512]; o += 512
    bk = w_in[:, o:o + 512]; o += 512
    bv = w_in[:, o:o + 512]; o += 512
    cols = []
    for h in range(A_HEADS):
        qh = aq[:, 64 * h:64 * h + 64]
        cols += [qh, z64] if h // (A_HEADS // A_KV_HEADS) == 0 else [z64, qh]
    cols += [ak, av, ik, iw, jnp.zeros((d, 60), w_in.dtype)]
    for h in range(IDX_HEADS):
        cols += [iq[:, 64 * h:64 * h + 64], z64]
    for j in range(2 * B_HEADS):
        qj = bq[:, 64 * j:64 * j + 64]
        cols += [qj, z64] if j % 2 == 0 else [z64, qj]
    cols += [bk, bv]
    w = jnp.concatenate(cols, axis=1)
    assert w.shape[1] == PROJ_WIDTH
    return w.astype(BF16)


def _proj_body(x_ref, g_ref, w_ref, cos_ref, sin_ref, qa_ref, qi_ref, bq_ref, *kv_refs, prompt):
    if prompt:
        kat32_ref, vat32_ref, ikt32_ref, bk32_ref, bv32_ref, ka16_ref, ik16_ref, bk16_ref, vat16_ref, bvt16_ref = kv_refs
    else:
        ka32_ref, va32_ref, ik32_ref, bk32_ref, bv32_ref = kv_refs
    h = _rms(x_ref[...], g_ref[...]).astype(BF16)
    cos, sin = cos_ref[...], sin_ref[...]
    lane = lax.broadcasted_iota(I32, cos.shape, 1)
    first_half = (lane & 63) < 32

    def rope(y):
        rot = jnp.where(first_half, pltpu.roll(y, LANES - 32, 1), pltpu.roll(y, 32, 1))
        return y * cos + rot * sin

    def seg(name):
        c0, width = PROJ_COLS[name]
        return _dot(h, w_ref[:, c0:c0 + width])

    def groups(y):
        return [y[:, LANES * j:LANES * (j + 1)] for j in range(y.shape[1] // LANES)]

    for j, y in enumerate(groups(seg("aq"))):
        qa_ref[:, LANES * j:LANES * (j + 1)] = (rope(y) * 0.125).astype(BF16)
    for j, y in enumerate(groups(seg("iq"))):
        qi_ref[:, LANES * j:LANES * (j + 1)] = (rope(y) * 0.125).astype(BF16)
    for j, y in enumerate(groups(seg("bq"))):
        bq_ref[:, LANES * j:LANES * (j + 1)] = (rope(y) * 0.125).astype(BF16)
    ka = rope(seg("ak"))
    va = seg("av")
    y = seg("ik")
    ik = jnp.where(lane < 64, rope(y), y)
    bk = [rope(y) for y in groups(seg("bk"))]
    bv = groups(seg("bv"))
    if not prompt:
        ka32_ref[...] = ka
        va32_ref[...] = va
        ik32_ref[...] = ik
        for j in range(B_HEADS):
            bk32_ref[:, LANES * j:LANES * (j + 1)] = bk[j]
            bv32_ref[:, LANES * j:LANES * (j + 1)] = bv[j]
        return
    tm = ka.shape[0]
    kat32_ref[...] = ka.T
    ka16_ref[...] = ka.astype(BF16)
    vat = va.T
    vat32_ref[...] = vat
    vat16_ref[...] = vat.astype(BF16)
    ikt32_ref[...] = ik.T
    ik16_ref[...] = ik.astype(BF16)
    for j in range(B_HEADS):
        rows = pl.ds(j, tm, stride=B_HEADS)
        bk32_ref[rows, :] = bk[j]
        bv32_ref[rows, :] = bv[j]
        bk16_ref[:, LANES * j:LANES * (j + 1)] = bk[j].astype(BF16)
        bvt16_ref[LANES * j:LANES * (j + 1), :] = bv[j].T.astype(BF16)


def _proj(x, g, w, cos, sin, *, prompt, n_batch=1):
    rows, d = x.shape
    seq = cos.shape[0]
    tm = _row_tile(seq)
    n_t = seq // tm
    row = lambda width: pl.BlockSpec((tm, width), lambda i: (i, 0))
    tab = pl.BlockSpec((tm, LANES), lambda i: (i % n_t, 0))
    rm = lambda width, dt: (row(width), jax.ShapeDtypeStruct((rows, width), dt))
    outs = dict(qa=rm(1024, BF16), qi=rm(512, BF16), bq=rm(1024, BF16))
    if prompt:
        fm = lambda feat, dt: (pl.BlockSpec((None, feat, tm), lambda i: (i // n_t, 0, i % n_t)),
                               jax.ShapeDtypeStruct((n_batch, feat, seq), dt))
        tiled = lambda feat, dt: (pl.BlockSpec((None, feat, tm), lambda i: (i, 0, 0)),
                                  jax.ShapeDtypeStruct((rows // tm, feat, tm), dt))
        head_rows = (pl.BlockSpec((tm * B_HEADS, LANES), lambda i: (i, 0)),
                     jax.ShapeDtypeStruct((rows * B_HEADS, LANES), F32))
        outs.update(kat32=fm(LANES, F32), vat32=fm(LANES, F32), ikt32=fm(LANES, F32), bk32=head_rows,
                    bv32=head_rows, ka16=rm(LANES, BF16), ik16=rm(LANES, BF16), bk16=rm(512, BF16),
                    vat16=tiled(LANES, BF16), bvt16=tiled(512, BF16))
    else:
        outs.update(ka32=rm(LANES, F32), va32=rm(LANES, F32), ik32=rm(LANES, F32), bk32=rm(512, F32),
                    bv32=rm(512, F32))
    out = pl.pallas_call(
        functools.partial(_proj_body, prompt=prompt), grid=(rows // tm,),
        in_specs=[row(d), _resident((1, d)), _resident(w.shape), tab, tab],
        out_specs=[spec for spec, _ in outs.values()],
        out_shape=[shape for _, shape in outs.values()],
        compiler_params=_cparams(("arbitrary",)), name="proj_prompt" if prompt else "proj_sample",
    )(x, g.reshape(1, d), w, cos, sin)
    return dict(zip(outs.keys(), out))


def _memkv_body(x_ref, g_ref, w_ref, k32_ref, v32_ref, k16_ref, v16_ref):
    h = _rms(x_ref[...], g_ref[...]).astype(BF16)
    y = _dot(h, w_ref[...])
    half = y.shape[1] // 2
    k, v = y[:, :half], y[:, half:]
    k32_ref[...] = k
    v32_ref[...] = v
    k16_ref[...] = k.astype(BF16)
    v16_ref[...] = v.astype(BF16)


def _memkv(x, g, w):
    rows, d = x.shape
    half = w.shape[1] // 2
    tm = _row_tile(rows)
    row = lambda width: pl.BlockSpec((tm, width), lambda i: (i, 0))
    return pl.pallas_call(
        _memkv_body, grid=(rows // tm,),
        in_specs=[row(d), _resident((1, d)), _resident(w.shape)],
        out_specs=[row(half)] * 4,
        out_shape=[jax.ShapeDtypeStruct((rows, half), dt) for dt in (F32, F32, BF16, BF16)],
        compiler_params=_cparams(("arbitrary",)), name="memkv",
    )(x, g.reshape(1, d), w)


def _float_key(s):
    bits = pltpu.bitcast(s + 0.0, I32)
    return bits ^ ((bits >> 31) & 0x7FFFFFFF)


def _kth_largest_key(count_ge, rows_shape, k):
    base = jnp.where(count_ge(jnp.zeros(rows_shape, I32)) >= k, 0, INT_MIN).astype(I32)

    def bit_step(i, base):
        cand = base | (jnp.int32(1) << (30 - i))
        return jnp.where(count_ge(cand) >= k, cand, base)

    return lax.fori_loop(0, 31, bit_step, base)


def _softmax_step(s, m_prev, l_prev, axis=-1):
    m_new = jnp.maximum(m_prev, jnp.max(s, axis=axis, keepdims=True))
    alpha = jnp.exp(m_prev - m_new)
    p = jnp.exp(s - m_new)
    l_new = alpha * l_prev + jnp.sum(p, axis=axis, keepdims=True)
    return p, m_new, l_new, alpha


def _dsa_body(qa_ref, qi_ref, ikt_ref, ka_ref, ik_ref, vat_ref, tril_ref, o_ref, key_scr, acc_scr,
              *, tq, tk, top_k):
    q0 = pl.program_id(1) * tq
    n_kb = (q0 + tq + tk - 1) // tk
    rep = A_HEADS // A_KV_HEADS
    q_pos = q0 + lax.broadcasted_iota(I32, (1, tq), 1)
    k_iota = lax.broadcasted_iota(I32, (tk, 1), 0)

    qi = jnp.concatenate([qi_ref[:, LANES * h:LANES * (h + 1)] for h in range(IDX_HEADS)], axis=0)
    w_heads = [ikt_ref[IDX_DIM + h:IDX_DIM + h + 1, :] for h in range(IDX_HEADS)]

    def score_block(kb, carry):
        k0 = pl.multiple_of(kb * tk, tk)
        logits = _dot_nt(ik_ref[pl.ds(k0, tk), :], qi)
        s = jnp.zeros((tk, tq), F32)
        for h in range(IDX_HEADS):
            s = s + jnp.maximum(logits[:, h * tq:(h + 1) * tq], 0.0) * w_heads[h]
        s = jnp.where(k0 + k_iota <= q_pos, s, -jnp.inf)
        key_scr[kb] = _float_key(s)
        return carry

    lax.fori_loop(0, n_kb, score_block, 0)

    def count_ge(cand):
        def blk(kb, acc):
            c = jnp.where(key_scr[kb] >= cand, 1, 0)
            return acc + jnp.sum(c.reshape(tk // 8, 8, tq), axis=0)
        acc = lax.fori_loop(0, n_kb, blk, jnp.zeros((8, tq), I32))
        return jnp.sum(acc, axis=0, keepdims=True)

    thr = _kth_largest_key(count_ge, (1, tq), top_k)
    need = (top_k - count_ge(thr + 1)).astype(F32)
    thr_tie = jnp.where(thr == KEY_NEG_INF, jnp.int32(2 ** 31 - 1), thr)

    qa = jnp.concatenate([qa_ref[:, LANES * h:LANES * (h + 1)] for h in range(A_HEADS)], axis=0)
    acc_scr[...] = jnp.zeros(acc_scr.shape, F32)
    group_cols = [slice(g * rep * tq, (g + 1) * rep * tq) for g in range(A_KV_HEADS)]

    def attend_block(kb, carry):
        m, l, tie_seen = carry
        k0 = pl.multiple_of(kb * tk, tk)
        key = key_scr[kb]
        tie = key == thr_tie
        rank = tie_seen + _dot(tril_ref[...], jnp.where(tie, 1.0, 0.0).astype(BF16))
        bias = jnp.where(key > thr, 0.0, jnp.where(tie, jnp.where(rank <= need, 0.0, NEG_BIG), NEG_BIG))
        s = _dot_nt(ka_ref[pl.ds(k0, tk), :], qa)
        s = jnp.concatenate([s[:, h * tq:(h + 1) * tq] + bias for h in range(A_HEADS)], axis=1)
        p, m, l, alpha = _softmax_step(s, m, l, axis=0)
        p = p.astype(BF16)
        vt = vat_ref[kb]
        for g in range(A_KV_HEADS):
            v_g = vt[A_HEAD_DIM * g:A_HEAD_DIM * (g + 1), :]
            acc_scr[g] = alpha[:, group_cols[g]] * acc_scr[g] + _dot(v_g, p[:, group_cols[g]])
        return m, l, rank[tk - 1:tk, :]

    stats = (1, A_HEADS * tq)
    _, l, _ = lax.fori_loop(0, n_kb, attend_block,
                            (jnp.full(stats, NEG_BIG, F32), jnp.zeros(stats, F32), jnp.zeros((1, tq), F32)))

    for g in range(A_KV_HEADS):
        o = acc_scr[g] / l[:, group_cols[g]]
        for j in range(rep // 2):
            pair = jnp.concatenate([o[:, (2 * j) * tq:(2 * j + 1) * tq], o[:, (2 * j + 1) * tq:(2 * j + 2) * tq]], axis=0)
            c = g * (rep // 2) + j
            o_ref[:, LANES * c:LANES * (c + 1)] = pair.T.astype(BF16)


def _dsa_prompt(p, n_batch, seq, top_k):
    tq = min(128, seq)
    tk = p["vat16"].shape[2]
    rep = A_HEADS // A_KV_HEADS
    n_q = seq // tq
    tril = jnp.tril(jnp.ones((tk, tk), BF16))
    qrow = lambda width: pl.BlockSpec((tq, width), lambda b, i: (b * n_q + i, 0))
    seqblk = lambda width: pl.BlockSpec((seq, width), lambda b, i: (b, 0))
    return pl.pallas_call(
        functools.partial(_dsa_body, tq=tq, tk=tk, top_k=top_k),
        grid=(n_batch, n_q),
        in_specs=[qrow(1024), qrow(512), pl.BlockSpec((None, LANES, tq), lambda b, i: (b, 0, i)),
                  seqblk(LANES), seqblk(LANES), pl.BlockSpec((seq // tk, LANES, tk), lambda b, i: (b, 0, 0)),
                  _resident((tk, tk))],
        out_specs=qrow(512),
        out_shape=jax.ShapeDtypeStruct((n_batch * seq, 512), BF16),
        scratch_shapes=[pltpu.VMEM((seq // tk, tk, tq), I32),
                        pltpu.VMEM((A_KV_HEADS, A_HEAD_DIM, rep * tq), F32)],
        compiler_params=_cparams(("arbitrary", "arbitrary")), name="dsa_prompt",
    )(p["qa"], p["qi"], p["ikt32"], p["ka16"], p["ik16"], p["vat16"], tril)


def _diff_lambda(lq1, lk1, lq2, lk2, lambda_init):
    s1 = jnp.sum(lq1 * lk1, axis=-1, keepdims=True)
    s2 = jnp.sum(lq2 * lk2, axis=-1, keepdims=True)
    return jnp.exp(s1) - jnp.exp(s2) + lambda_init


def _subln(o, gain, lambda_init):
    return _rms(o, gain) * (1.0 - lambda_init)


def _diff_body(bq_ref, bk_ref, bvt_ref, lq1_ref, lk1_ref, lq2_ref, lk2_ref, gain_ref, o_ref, acc_scr,
               *, tq, tk, lambda_init):
    q0 = pl.program_id(1) * tq
    n_full = q0 // tk
    qp = q0 + lax.broadcasted_iota(I32, (1, tq), 1)
    q_pos = jnp.concatenate([qp, qp], axis=1)
    k_iota = lax.broadcasted_iota(I32, (tk, 1), 0)
    acc_scr[...] = jnp.zeros(acc_scr.shape, F32)
    q_heads = [jnp.concatenate([bq_ref[:, LANES * (2 * h + c):LANES * (2 * h + c + 1)] for c in range(2)], axis=0)
               for h in range(B_HEADS)]

    def block(kb, stats, masked):
        k0 = pl.multiple_of(kb * tk, tk)
        vt = bvt_ref[kb]
        out = []
        for h in range(B_HEADS):
            m, l = stats[2 * h], stats[2 * h + 1]
            s = _dot_nt(bk_ref[pl.ds(k0, tk), LANES * h:LANES * (h + 1)], q_heads[h])
            if masked:
                s = jnp.where(k0 + k_iota <= q_pos, s, NEG_BIG)
            p, m, l, alpha = _softmax_step(s, m, l, axis=0)
            acc_scr[h] = alpha * acc_scr[h] + _dot(vt[LANES * h:LANES * (h + 1), :], p.astype(BF16))
            out += [m, l]
        return tuple(out)

    shape = (1, 2 * tq)
    init = tuple(jnp.full(shape, NEG_BIG, F32) if i % 2 == 0 else jnp.zeros(shape, F32) for i in range(2 * B_HEADS))
    stats = lax.fori_loop(0, n_full, lambda kb, st: block(kb, st, False), init)
    for d in range(-(-tq // tk)):
        stats = block(n_full + d, stats, True)

    lam = _diff_lambda(lq1_ref[...], lk1_ref[...], lq2_ref[...], lk2_ref[...], lambda_init)
    for h in range(B_HEADS):
        o = acc_scr[h] / stats[2 * h + 1]
        o = (o[:, :tq] - lam * o[:, tq:]).T
        o_ref[:, LANES * h:LANES * (h + 1)] = _subln(o, gain_ref[...], lambda_init).astype(BF16)


def _diff_prompt(p, lam_params, gain, n_batch, seq, lambda_init):
    tk = p["bvt16"].shape[2]
    tq = min(256, seq)
    assert tk % tq == 0 or tq % tk == 0
    n_q = seq // tq
    qrow = lambda width: pl.BlockSpec((tq, width), lambda b, i: (b * n_q + i, 0))
    small = [_resident((1, a.shape[-1])) for a in lam_params] + [_resident((1, B_V_DIM))]
    return pl.pallas_call(
        functools.partial(_diff_body, tq=tq, tk=tk, lambda_init=lambda_init),
        grid=(n_batch, n_q),
        in_specs=[qrow(1024), pl.BlockSpec((seq, 512), lambda b, i: (b, 0)),
                  pl.BlockSpec((seq // tk, 512, tk), lambda b, i: (b, 0, 0))] + small,
        out_specs=qrow(512),
        out_shape=jax.ShapeDtypeStruct((n_batch * seq, 512), BF16),
        scratch_shapes=[pltpu.VMEM((B_HEADS, B_V_DIM, 2 * tq), F32)],
        compiler_params=_cparams(("arbitrary", "arbitrary")), name="diff_prompt",
    )(p["bq"], p["bk16"], p["bvt16"], *[a.reshape(1, -1) for a in lam_params], gain.reshape(1, -1))


def _post_body(*refs, mem_in_kernel):
    if mem_in_kernel:
        (x_ref, g_ref, oa_ref, ob_ref, mk_ref, mv_ref, wmq_ref, wg_ref, wpa_ref, wpb_ref, wpm_ref, wo_ref,
         o_ref) = refs
    else:
        x_ref, g_ref, oa_ref, ob_ref, om_ref, wg_ref, wpa_ref, wpb_ref, wpm_ref, wo_ref, o_ref = refs
    x = x_ref[...]
    d = x.shape[1]
    h = _rms(x, g_ref[...]).astype(BF16)
    if mem_in_kernel:
        mq = (_dot(h, wmq_ref[...]) * (M_HEAD_DIM ** -0.5)).astype(BF16)
        heads = []
        for hd in range(M_HEADS):
            sl = slice(LANES * hd, LANES * (hd + 1))
            s = _dot_nt(mq[:, sl], mk_ref[:, sl])
            e = jnp.exp(s - jnp.max(s, axis=-1, keepdims=True))
            o = _dot(e.astype(BF16), mv_ref[:, sl]) / jnp.sum(e, axis=-1, keepdims=True)
            heads.append(o.astype(BF16))
        om = jnp.concatenate(heads, axis=1)
    else:
        om = om_ref[...]

    def gate(i):
        return jax.nn.sigmoid(_dot(h, wg_ref[:, d * i:d * (i + 1)]))

    merged = gate(0) * _dot(oa_ref[...], wpa_ref[...])
    merged = merged + gate(1) * _dot(ob_ref[...], wpb_ref[...])
    merged = merged + gate(2) * _dot(om, wpm_ref[...])
    o_ref[...] = x + _dot(merged.astype(BF16), wo_ref[...])


def _post(x, g, oa, ob, w, *, mem=None, om=None, n_batch=1):
    rows, d = x.shape
    seq = rows // n_batch
    tm = _row_tile(seq)
    n_t = seq // tm
    row = lambda width: pl.BlockSpec((tm, width), lambda b, i: (b * n_t + i, 0))
    weights = [w["gates"], w["proj_a"], w["proj_b"], w["proj_m"], w["out"]]
    if mem is not None:
        mk, mv, w_mq = mem
        memblk = pl.BlockSpec((MEM_TOKENS, mk.shape[1]), lambda b, i: (b, 0))
        args = [x, g.reshape(1, d), oa, ob, mk, mv, w_mq] + weights
        in_specs = ([row(d), _resident((1, d)), row(512), row(512), memblk, memblk, _resident(w_mq.shape)]
                    + [_resident(a.shape) for a in weights])
    else:
        args = [x, g.reshape(1, d), oa, ob, om] + weights
        in_specs = [row(d), _resident((1, d)), row(512), row(512), row(512)] + [_resident(a.shape) for a in weights]
    return pl.pallas_call(
        functools.partial(_post_body, mem_in_kernel=mem is not None),
        grid=(n_batch, n_t), in_specs=in_specs, out_specs=row(d),
        out_shape=jax.ShapeDtypeStruct((rows, d), F32),
        compiler_params=_cparams(("arbitrary", "arbitrary")), name="post",
    )(*args)


def _mem_sample_body(x_ref, g_ref, wmq_ref, mk_ref, mv_ref, o_ref, mq_scr):
    b = pl.program_id(0)

    @pl.when(b == 0)
    def _():
        h = _rms(x_ref[...], g_ref[...]).astype(BF16)
        mq_scr[...] = _dot(h, wmq_ref[...]) * (M_HEAD_DIM ** -0.5)

    q = mq_scr[pl.ds(b, 1), :]
    for hd in range(M_HEADS):
        sl = slice(LANES * hd, LANES * (hd + 1))
        rows = pl.ds(hd, MEM_TOKENS, stride=M_HEADS)
        s = jnp.sum(mk_ref[rows, :] * q[:, sl], axis=-1, keepdims=True)
        e = jnp.exp(s - jnp.max(s, axis=0, keepdims=True))
        o = jnp.sum(e * mv_ref[rows, :], axis=0, keepdims=True) / jnp.sum(e, axis=0, keepdims=True)
        o_ref[:, sl] = o.astype(BF16)


def _mem_sample(x, g, w_mq, cache_k, cache_v, layer):
    n, d = x.shape
    width = w_mq.shape[1]
    memblk = pl.BlockSpec((None, None) + cache_k.shape[2:], lambda b: (b, layer, 0, 0))
    return pl.pallas_call(
        _mem_sample_body, grid=(n,),
        in_specs=[_resident((n, d)), _resident((1, d)), _resident(w_mq.shape), memblk, memblk],
        out_specs=pl.BlockSpec((None, 1, width), lambda b: (b, 0, 0)),
        out_shape=jax.ShapeDtypeStruct((n, 1, width), BF16),
        scratch_shapes=[pltpu.VMEM((n, width), F32)],
        compiler_params=_cparams(("arbitrary",)), name="mem_sample",
    )(x, g.reshape(1, d), w_mq, cache_k, cache_v)


def _page_copies(pt_ref, b, chunk, pages_per_chunk, layer, streams):
    slot = chunk % 2
    out = []
    for cache, dst, sem in streams:
        for j in range(pages_per_chunk):
            page = pt_ref[b, chunk * pages_per_chunk + j]
            out.append(pltpu.make_async_copy(cache.at[page, layer], dst(slot, j), sem.at[slot]))
    return out


def _lane_pages(buf, page):
    return lambda slot, j: buf.at[slot, :, pl.ds(page * j, page)]


def _row_pages(buf, rows):
    return lambda slot, j: buf.at[slot, pl.ds(rows * j, rows), :]


def _chunks(n_pages):
    ppc = min(16, n_pages)
    assert n_pages % ppc == 0
    return ppc, n_pages // ppc


def _for_each_chunk(pt_ref, n_pages, layer, streams, body):
    b = pl.program_id(0)
    n_samples = pl.num_programs(0)
    ppc, n_ch = _chunks(n_pages)
    across_samples = n_ch % 2 == 0

    def start(sample, c):
        for cp in _page_copies(pt_ref, sample, c, ppc, layer, streams):
            cp.start()

    if across_samples:
        pl.when(b == 0)(lambda: start(b, 0))
    else:
        start(b, 0)
    for c in range(n_ch):
        if c + 1 < n_ch:
            start(b, c + 1)
        elif across_samples:
            pl.when(b + 1 < n_samples)(lambda: start(b + 1, 0))
        for cp in _page_copies(pt_ref, b, c, ppc, layer, streams):
            cp.wait()
        body(c)


def _idx_sample_body(pt_ref, qi_ref, w_ref, ikn_ref, cache_ref, s_ref, buf, sem, *, layer, n_pages, page):
    ppc, n_ch = _chunks(n_pages)
    ck = ppc * page
    q = qi_ref[:, 0:IDX_DIM]
    w = w_ref[...]

    def scores(logits):
        return jnp.sum(jnp.maximum(logits, 0.0) * w, axis=0, keepdims=True)

    def chunk(c):
        s_ref[:, ck * c:ck * (c + 1)] = scores(_dot(q, buf[c % 2].astype(BF16)))

    _for_each_chunk(pt_ref, n_pages, layer, [(cache_ref, _lane_pages(buf, page), sem)], chunk)
    lane = lax.broadcasted_iota(I32, (1, LANES), 1)
    new_key = ikn_ref[:, 0:IDX_DIM].astype(BF16).astype(F32)
    s_new = scores(jnp.sum(q.astype(F32) * new_key, axis=1, keepdims=True))
    s_ref[:, ck * n_ch:] = jnp.where(lane == 0, s_new, -jnp.inf)


def _idx_sample(page_table, qi8, w8, ik_new, cache_t, layer):
    n, n_pages = page_table.shape
    feat, page = cache_t.shape[2], cache_t.shape[3]
    ppc, n_ch = _chunks(n_pages)
    nk = n_pages * page + LANES
    grid_spec = pltpu.PrefetchScalarGridSpec(
        num_scalar_prefetch=1, grid=(n,),
        in_specs=[pl.BlockSpec((None, 8, LANES), lambda b, pt: (b, 0, 0)),
                  pl.BlockSpec((None, 8, 1), lambda b, pt: (b, 0, 0)),
                  pl.BlockSpec((None, 1, LANES), lambda b, pt: (b, 0, 0)),
                  pl.BlockSpec(memory_space=pl.ANY)],
        out_specs=pl.BlockSpec((None, 1, nk), lambda b, pt: (b, 0, 0)),
        scratch_shapes=[pltpu.VMEM((2, feat, ppc * page), F32), pltpu.SemaphoreType.DMA((2,))])
    return pl.pallas_call(
        functools.partial(_idx_sample_body, layer=layer, n_pages=n_pages, page=page),
        grid_spec=grid_spec, out_shape=jax.ShapeDtypeStruct((n, 1, nk), F32),
        compiler_params=_cparams(("arbitrary",)), name="idx_sample",
    )(page_table, qi8, w8, ik_new, cache_t)


def _select_body(s_ref, bias_ref, *, top_k, n_valid):
    key = _float_key(s_ref[...])
    rows, nk = key.shape
    shape = (rows, 1)

    def count(ones):
        return jnp.sum(ones, axis=1, keepdims=True)

    thr = _kth_largest_key(lambda cand: count(jnp.where(key >= cand, 1, 0)), shape, top_k)
    need = top_k - count(jnp.where(key > thr, 1, 0))
    thr_tie = jnp.where(thr == KEY_NEG_INF, jnp.int32(2 ** 31 - 1), thr)
    tie = jnp.where(key == thr_tie, 1, 0)
    idx = lax.broadcasted_iota(I32, key.shape, 1)
    n_bits = max(1, (nk - 1).bit_length())

    def bit_step(i, last):
        cand = last - (jnp.int32(1) << (n_bits - 1 - i))
        return jnp.where(count(jnp.where(idx <= cand, tie, 0)) >= need, cand, last)

    last = lax.fori_loop(0, n_bits, bit_step, jnp.full(shape, 2 ** n_bits - 1, I32))
    tie_bias = jnp.where(idx <= last, jnp.where(tie > 0, 0.0, NEG_BIG), NEG_BIG)
    bias = jnp.where(key > thr, 0.0, tie_bias)
    bias_ref[...] = jnp.where(idx < n_valid, bias, NEG_BIG)


def _select(scores, top_k, n_valid):
    return pl.pallas_call(
        functools.partial(_select_body, top_k=top_k, n_valid=n_valid),
        out_shape=jax.ShapeDtypeStruct(scores.shape, F32),
        compiler_params=pltpu.CompilerParams(vmem_limit_bytes=V7X_VMEM_LIMIT_BYTES), name="select_sample",
    )(scores)


def _bf16_round(x):
    return x.astype(BF16).astype(F32)


def _new_row_step(s_new, v_new, m, l, acc):
    m_new = jnp.maximum(m, s_new)
    alpha = jnp.exp(m - m_new)
    p_new = jnp.exp(s_new - m_new)
    return alpha * l + p_new, alpha * acc + p_new * v_new


def _dsa_sample_body(pt_ref, q_ref, kn_ref, vn_ref, bias_ref, kc_ref, vc_ref, o_ref, kbuf, vbuf, ksem, vsem,
                     *, layer, n_pages, page):
    ppc, n_ch = _chunks(n_pages)
    ck = ppc * page
    q = q_ref[...]
    st = dict(m=jnp.full((8, 1), NEG_BIG, F32), l=jnp.zeros((8, 1), F32), acc=jnp.zeros((8, LANES), F32))

    def chunk(c):
        s = _dot(q, kbuf[c % 2].astype(BF16)) + bias_ref[:, ck * c:ck * (c + 1)]
        p, st["m"], st["l"], alpha = _softmax_step(s, st["m"], st["l"])
        st["acc"] = alpha * st["acc"] + _dot_nt(p.astype(BF16), vbuf[c % 2].astype(BF16))

    _for_each_chunk(pt_ref, n_pages, layer,
                    [(kc_ref, _lane_pages(kbuf, page), ksem), (vc_ref, _lane_pages(vbuf, page), vsem)], chunk)
    s_new = (jnp.sum(q.astype(F32) * _bf16_round(kn_ref[...]), axis=1, keepdims=True)
             + bias_ref[:, ck * n_ch:ck * n_ch + 1])
    l, acc = _new_row_step(s_new, _bf16_round(vn_ref[...]), st["m"], st["l"], st["acc"])
    o_ref[...] = acc / l


def _diff_sample_body(pt_ref, q_ref, kn_ref, vn_ref, kc_ref, vc_ref, lq1_ref, lk1_ref, lq2_ref, lk2_ref,
                      gain_ref, o_ref, kbuf, vbuf, ksem, vsem, *, layer, n_pages, page, lambda_init):
    ppc, n_ch = _chunks(n_pages)
    ck = ppc * page
    page_rows = page * B_HEADS
    q = q_ref[...]
    row_head = lax.broadcasted_iota(I32, (8, 1), 0) >> 1
    st = dict(m=jnp.full((8, 1), NEG_BIG, F32), l=jnp.zeros((8, 1), F32), acc=jnp.zeros((8, B_HEADS * LANES), F32))

    def chunk(c):
        head_rows = lambda buf, h: buf[c % 2, pl.ds(h, ck, stride=B_HEADS), :].astype(BF16)
        s = jnp.zeros((8, ck), F32)
        for h in range(B_HEADS):
            s = jnp.where(row_head == h, _dot_nt(q, head_rows(kbuf, h)), s)
        p, st["m"], st["l"], alpha = _softmax_step(s, st["m"], st["l"])
        pb = p.astype(BF16)
        pv = jnp.concatenate([_dot(pb, head_rows(vbuf, h)) for h in range(B_HEADS)], axis=1)
        st["acc"] = alpha * st["acc"] + pv

    _for_each_chunk(pt_ref, n_pages, layer,
                    [(kc_ref, _row_pages(kbuf, page_rows), ksem), (vc_ref, _row_pages(vbuf, page_rows), vsem)], chunk)
    qf = q.astype(F32)
    kn = _bf16_round(kn_ref[...])
    s_new = jnp.zeros((8, 1), F32)
    for h in range(B_HEADS):
        s_h = jnp.sum(qf * kn[:, LANES * h:LANES * (h + 1)], axis=1, keepdims=True)
        s_new = jnp.where(row_head == h, s_h, s_new)
    l, acc = _new_row_step(s_new, _bf16_round(vn_ref[...]), st["m"], st["l"], st["acc"])
    o = acc / l
    lam = _diff_lambda(lq1_ref[...], lk1_ref[...], lq2_ref[...], lk2_ref[...], lambda_init)
    for h in range(B_HEADS):
        sl = slice(LANES * h, LANES * (h + 1))
        oh = o[2 * h:2 * h + 1, sl] - lam * o[2 * h + 1:2 * h + 2, sl]
        o_ref[:, sl] = _subln(oh, gain_ref[...], lambda_init).astype(BF16)


def _per_sample(a):
    return pl.BlockSpec((None,) + a.shape[1:], lambda b, pt: (b,) + (0,) * (a.ndim - 1))


def _dsa_sample(page_table, q8, k_new, v_new, bias, cache_kt, cache_vt, layer):
    n, n_pages = page_table.shape
    feat, page = cache_kt.shape[2], cache_kt.shape[3]
    ppc, _ = _chunks(n_pages)
    any_spec = pl.BlockSpec(memory_space=pl.ANY)
    grid_spec = pltpu.PrefetchScalarGridSpec(
        num_scalar_prefetch=1, grid=(n,),
        in_specs=[_per_sample(q8), _per_sample(k_new), _per_sample(v_new), _per_sample(bias), any_spec, any_spec],
        out_specs=pl.BlockSpec((None, 8, feat), lambda b, pt: (b, 0, 0)),
        scratch_shapes=[pltpu.VMEM((2, feat, ppc * page), F32), pltpu.VMEM((2, feat, ppc * page), F32),
                        pltpu.SemaphoreType.DMA((2,)), pltpu.SemaphoreType.DMA((2,))])
    return pl.pallas_call(
        functools.partial(_dsa_sample_body, layer=layer, n_pages=n_pages, page=page),
        grid_spec=grid_spec, out_shape=jax.ShapeDtypeStruct((n, 8, feat), F32),
        compiler_params=_cparams(("arbitrary",)), name="dsa_sample",
    )(page_table, q8, k_new, v_new, bias, cache_kt, cache_vt)


def _diff_sample(page_table, q8, k_new, v_new, cache_k, cache_v, layer, diff_params, lambda_init):
    n, n_pages = page_table.shape
    page_rows = cache_k.shape[2]
    ppc, _ = _chunks(n_pages)
    width = B_HEADS * B_V_DIM
    any_spec = pl.BlockSpec(memory_space=pl.ANY)
    extra = [a.reshape(1, -1) for a in diff_params]
    grid_spec = pltpu.PrefetchScalarGridSpec(
        num_scalar_prefetch=1, grid=(n,),
        in_specs=([_per_sample(q8), _per_sample(k_new), _per_sample(v_new), any_spec, any_spec]
                  + [pl.BlockSpec(a.shape, lambda b, pt: (0, 0)) for a in extra]),
        out_specs=pl.BlockSpec((None, 1, width), lambda b, pt: (b, 0, 0)),
        scratch_shapes=[pltpu.VMEM((2, ppc * page_rows, LANES), F32), pltpu.VMEM((2, ppc * page_rows, LANES), F32),
                        pltpu.SemaphoreType.DMA((2,)), pltpu.SemaphoreType.DMA((2,))])
    return pl.pallas_call(
        functools.partial(_diff_sample_body, layer=layer, n_pages=n_pages, page=page_rows // B_HEADS,
                          lambda_init=lambda_init),
        grid_spec=grid_spec, out_shape=jax.ShapeDtypeStruct((n, 1, width), BF16),
        compiler_params=_cparams(("arbitrary",)), name="diff_sample",
    )(page_table, q8, k_new, v_new, cache_k, cache_v, *extra)


def _layer_weights(l, w_in, w_proj_a, w_proj_b, w_proj_m, w_out):
    o_mq = sum((512, 128, 128, 256, 64, 4, 512, 512, 512))
    return dict(proj=_proj_weight(w_in[l]),
                mq=w_in[l][:, o_mq:o_mq + 512].astype(BF16),
                gates=w_in[l][:, o_mq + 512:].astype(BF16),
                proj_a=w_proj_a[l].astype(BF16), proj_b=w_proj_b[l].astype(BF16),
                proj_m=w_proj_m[l].astype(BF16), out=w_out[l].astype(BF16))


def kernel(x_prompt, x_sample, cache_a_k, cache_a_v, cache_idx_k, cache_b_k, cache_b_v, cache_mem_k, cache_mem_v, page_table, mem_prompt, norm_ffn1, ffn1_w_in, ffn1_w_out, norm_mix, w_in, lambda_q1, lambda_k1, lambda_q2, lambda_k2, subln_gain, norm_mem, w_mem_kv, w_proj_a, w_proj_b, w_proj_m, w_out, norm_ffn2, ffn2_w_in, ffn2_w_out, final_norm):
    n_b, seq, d = x_prompt.shape
    n_s, dec_seq, _ = x_sample.shape
    assert dec_seq == 1
    depth = norm_ffn1.shape[0]
    n_pool, _, page = cache_a_k.shape[:3]
    n_pages = page_table.shape[1]
    past_len = n_pages * page
    top_k_prompt = min(IDX_TOPK_MAX, seq // 4)
    top_k_sample = min(IDX_TOPK_MAX, (past_len + dec_seq) // 4)

    cos_p, sin_p = _rope_tables(jnp.arange(seq, dtype=I32))
    cos_s, sin_s = _rope_tables(jnp.full((n_s,), past_len, I32))
    c_ik = jnp.transpose(cache_idx_k, (0, 1, 3, 2))
    c_ak = jnp.transpose(cache_a_k, (0, 1, 3, 4, 2)).reshape(n_pool, depth, -1, page)
    c_av = jnp.transpose(cache_a_v, (0, 1, 3, 4, 2)).reshape(n_pool, depth, -1, page)
    c_bk = cache_b_k.reshape(n_pool, depth, page * B_HEADS, -1)
    c_bv = cache_b_v.reshape(n_pool, depth, page * B_HEADS, -1)
    c_mk = cache_mem_k.reshape(n_s, depth, MEM_TOKENS * M_HEADS, -1)
    c_mv = cache_mem_v.reshape(n_s, depth, MEM_TOKENS * M_HEADS, -1)
    mem_rows = mem_prompt.reshape(n_b * MEM_TOKENS, d)

    xp = x_prompt.reshape(n_b * seq, d)
    xs = x_sample.reshape(n_s, d)
    rows_p, rows_s, mem_rows_out = [], [], []
    yp = ys = None
    for l in range(depth):
        last = l == depth - 1
        lambda_init = 0.8 - 0.6 * math.exp(-0.3 * l)
        w = _layer_weights(l, w_in, w_proj_a, w_proj_b, w_proj_m, w_out)
        f1_in, f1_out = ffn1_w_in[l].astype(BF16), ffn1_w_out[l].astype(BF16)
        f2_in, f2_out = ffn2_w_in[l].astype(BF16), ffn2_w_out[l].astype(BF16)
        lam_params = (lambda_q1[l], lambda_k1[l], lambda_q2[l], lambda_k2[l])

        mk32, mv32, mk16, mv16 = _memkv(mem_rows, norm_mem[l], w_mem_kv[l].astype(BF16))
        mem_rows_out.append((mk32, mv32))
        xp = _ffn(xp, norm_ffn1[l], f1_in, f1_out)
        p = _proj(xp, norm_mix[l], w["proj"], cos_p, sin_p, prompt=True, n_batch=n_b)
        rows_p.append(p)
        oa = _dsa_prompt(p, n_b, seq, top_k_prompt)
        ob = _diff_prompt(p, lam_params, subln_gain[l], n_b, seq, lambda_init)
        xp = _post(xp, norm_mix[l], oa, ob, w, mem=(mk16, mv16, w["mq"]), n_batch=n_b)
        if last:
            xp, yp = _ffn(xp, norm_ffn2[l], f2_in, f2_out, final_norm)
        else:
            xp = _ffn(xp, norm_ffn2[l], f2_in, f2_out)

        xs = _ffn(xs, norm_ffn1[l], f1_in, f1_out)
        s = _proj(xs, norm_mix[l], w["proj"], cos_s, sin_s, prompt=False)
        rows_s.append(s)
        qi8 = jnp.pad(s["qi"].reshape(n_s, IDX_HEADS, LANES), ((0, 0), (0, 8 - IDX_HEADS), (0, 0)))
        w8 = jnp.pad(s["ik32"][:, IDX_DIM:IDX_DIM + IDX_HEADS], ((0, 0), (0, 8 - IDX_HEADS)))[:, :, None]
        scores = _idx_sample(page_table, qi8, w8, s["ik32"].reshape(n_s, 1, LANES), c_ik, l)
        bias = _select(scores.reshape(n_s, -1), top_k_sample, past_len + 1).reshape(n_s, 1, -1)
        oa8 = _dsa_sample(page_table, s["qa"].reshape(n_s, A_HEADS, LANES), s["ka32"].reshape(n_s, 1, -1),
                          s["va32"].reshape(n_s, 1, -1), bias, c_ak, c_av, l)
        rep = A_HEADS // A_KV_HEADS
        oa_s = jnp.concatenate([oa8[:, h, 64 * (h // rep):64 * (h // rep) + 64] for h in range(A_HEADS)],
                               axis=1).astype(BF16)
        ob_s = _diff_sample(page_table, s["bq"].reshape(n_s, 2 * B_HEADS, LANES), s["bk32"].reshape(n_s, 1, -1),
                            s["bv32"].reshape(n_s, 1, -1), c_bk, c_bv, l, lam_params + (subln_gain[l],),
                            lambda_init).reshape(n_s, -1)
        om_s = _mem_sample(xs, norm_mix[l], w["mq"], c_mk, c_mv, l).reshape(n_s, -1)
        xs = _post(xs, norm_mix[l], oa_s, ob_s, w, om=om_s)
        if last:
            xs, ys = _ffn(xs, norm_ffn2[l], f2_in, f2_out, final_norm)
        else:
            xs = _ffn(xs, norm_ffn2[l], f2_in, f2_out)

    def stack(rows, name, n, t, shape):
        width = int(math.prod(shape))
        return jnp.stack([r[name][:, :width].reshape(n, t, *shape) for r in rows], axis=1)

    def stack_fm(name, feat, shape):
        a = jnp.stack([r[name][:, :feat, :] for r in rows_p], axis=1)
        return jnp.moveaxis(a.reshape(n_b, depth, *shape, seq), -1, 2)

    def stack_mem(i):
        return jnp.stack([r[i].reshape(n_b, MEM_TOKENS, M_HEADS, M_HEAD_DIM) for r in mem_rows_out], axis=1)

    kv_p = (stack_fm("kat32", LANES, (A_KV_HEADS, A_HEAD_DIM)), stack_fm("vat32", LANES, (A_KV_HEADS, A_HEAD_DIM)),
            stack_fm("ikt32", IDX_DIM, (IDX_DIM,)),
            stack(rows_p, "bk32", n_b, seq, (B_HEADS, 2 * B_QK_DIM)), stack(rows_p, "bv32", n_b, seq, (B_HEADS, B_V_DIM)))
    kv_s = (stack(rows_s, "ka32", n_s, dec_seq, (A_KV_HEADS, A_HEAD_DIM)),
            stack(rows_s, "va32", n_s, dec_seq, (A_KV_HEADS, A_HEAD_DIM)),
            stack(rows_s, "ik32", n_s, dec_seq, (IDX_DIM,)),
            stack(rows_s, "bk32", n_s, dec_seq, (B_HEADS, 2 * B_QK_DIM)),
            stack(rows_s, "bv32", n_s, dec_seq, (B_HEADS, B_V_DIM)))
    return ((yp.reshape(n_b, seq, d), ys.reshape(n_s, dec_seq, d)) + kv_p + (stack_mem(0), stack_mem(1)) + kv_s)
```

```python
import functools
import math

import jax
import jax.numpy as jnp
from jax import lax
from jax.experimental import pallas as pl
from jax.experimental.pallas import tpu as pltpu

F32 = jnp.float32
BF16 = jnp.bfloat16
I32 = jnp.int32

A_HEADS, A_KV_HEADS, A_HEAD_DIM = 8, 2, 64
IDX_HEADS, IDX_DIM, IDX_TOPK_MAX = 4, 64, 256
B_HEADS, B_QK_DIM, B_V_DIM = 4, 64, 128
MEM_TOKENS, M_HEADS, M_HEAD_DIM = 256, 4, 128
N_BRANCH = 3
ROPE_THETA = 10000.0
RMS_EPS = 1e-6

LANES = 128
V7X_VMEM_LIMIT_BYTES = 60000 * 1024

NEG_BIG = -1e30
INT_MIN = -(2 ** 31)
KEY_NEG_INF = -(2 ** 31) + 0x7FFFFF

PROJ_COLS = dict(aq=(0, 1024), ak=(1024, 128), av=(1152, 128), ik=(1280, 128), iq=(1408, 512),
                 bq=(1920, 1024), bk=(2944, 512), bv=(3456, 512))
PROJ_WIDTH = 3968


def _cparams(sem):
    return pltpu.CompilerParams(dimension_semantics=sem, vmem_limit_bytes=V7X_VMEM_LIMIT_BYTES)


def _resident(shape):
    zeros = (0,) * len(shape)
    return pl.BlockSpec(shape, lambda *_: zeros, pipeline_mode=pl.Buffered(1))


def _rms(x, g):
    return x * lax.rsqrt(jnp.mean(x * x, axis=-1, keepdims=True) + RMS_EPS) * g


def _dot(a, b):
    return jnp.dot(a, b, preferred_element_type=F32)


def _dot_nt(a, b):
    return lax.dot_general(a, b, (((1,), (1,)), ((), ())), preferred_element_type=F32)


def _row_tile(rows, target=512):
    t = min(rows, target)
    assert rows % t == 0 and t % 8 == 0, (rows, t)
    return t


def _ffn_body(*refs, d_ff, chunk, final):
    if final:
        x_ref, g_ref, wi_ref, wo_ref, fg_ref, o_ref, of_ref = refs
    else:
        x_ref, g_ref, wi_ref, wo_ref, o_ref = refs
    x = x_ref[...]
    h = _rms(x, g_ref[...]).astype(BF16)
    acc = jnp.zeros_like(x)
    for c in range(d_ff // chunk):
        gate = _dot(h, wi_ref[:, c * chunk:(c + 1) * chunk])
        up = _dot(h, wi_ref[:, d_ff + c * chunk:d_ff + (c + 1) * chunk])
        act = (gate * jax.nn.sigmoid(gate) * up).astype(BF16)
        acc = acc + _dot(act, wo_ref[c * chunk:(c + 1) * chunk, :])
    y = x + 0.5 * acc
    o_ref[...] = y
    if final:
        of_ref[...] = _rms(y, fg_ref[...])


def _ffn(x, g, w_in, w_out, final_g=None):
    rows, d = x.shape
    d_ff = w_out.shape[0]
    chunk = d_ff // 2 if (d_ff // 2) % LANES == 0 else d_ff
    tm = _row_tile(rows)
    final = final_g is not None
    row_spec = pl.BlockSpec((tm, d), lambda i: (i, 0))
    in_specs = [row_spec, _resident((1, d)), _resident(w_in.shape), _resident(w_out.shape)]
    args = [x, g.reshape(1, d), w_in, w_out]
    out_shape = [jax.ShapeDtypeStruct((rows, d), F32)]
    out_specs = [row_spec]
    if final:
        in_specs.append(_resident((1, d)))
        args.append(final_g.reshape(1, d))
        out_shape.append(jax.ShapeDtypeStruct((rows, d), F32))
        out_specs.append(row_spec)
    out = pl.pallas_call(
        functools.partial(_ffn_body, d_ff=d_ff, chunk=chunk, final=final),
        grid=(rows // tm,), in_specs=in_specs, out_specs=out_specs, out_shape=out_shape,
        compiler_params=_cparams(("arbitrary",)), name="ffn_final" if final else "ffn",
    )(*args)
    return out if final else out[0]


def _rope_tables(pos):
    half = A_HEAD_DIM // 2
    inv = ROPE_THETA ** (-jnp.arange(half, dtype=F32) / half)
    ang = pos.astype(F32)[:, None] * inv[None, :]
    cos, sin = jnp.cos(ang), jnp.sin(ang)
    cos = jnp.tile(cos, (1, LANES // half))
    sin = jnp.tile(jnp.concatenate([-sin, sin], axis=1), (1, LANES // (2 * half)))
    return cos, sin


def _proj_weight(wt):
    d = wt.shape[1]
    z64 = jnp.zeros((64, d), wt.dtype)
    o = 0
    aq = wt[o:o + 512]; o += 512
    ak = wt[o:o + 128]; o += 128
    av = wt[o:o + 128]; o += 128
    iq = wt[o:o + 256]; o += 256
    ik = wt[o:o + 64]; o += 64
    iw = wt[o:o + 4]; o += 4
    bq = wt[o:o + 512]; o += 512
    bk = wt[o:o + 512]; o += 512
    bv = wt[o:o + 512]; o += 512
    rows = []
    for h in range(A_HEADS):
        qh = aq[64 * h:64 * h + 64]
        rows += [qh, z64] if h // (A_HEADS // A_KV_HEADS) == 0 else [z64, qh]
    rows += [ak, av, ik, iw, jnp.zeros((60, d), wt.dtype)]
    for h in range(IDX_HEADS):
        rows += [iq[64 * h:64 * h + 64], z64]
    for j in range(2 * B_HEADS):
        qj = bq[64 * j:64 * j + 64]
        rows += [qj, z64] if j % 2 == 0 else [z64, qj]
    rows += [bk, bv]
    w = jnp.concatenate(rows, axis=0)
    assert w.shape[0] == PROJ_WIDTH
    return w.astype(BF16)


def _proj_body(x_ref, g_ref, w_ref, cos_ref, sin_ref, qa_ref, qi_ref, bq_ref, *kv_refs, prompt):
    if prompt:
        kat32_ref, vat32_ref, ikt32_ref, bk32_ref, bv32_ref, ka16_ref, ik16_ref, bk16_ref, vat16_ref, bvt16_ref = kv_refs
    else:
        ka32_ref, va32_ref, ik32_ref, bk32_ref, bv32_ref = kv_refs
    h = _rms(x_ref[...], g_ref[...]).astype(BF16)
    cos, sin = cos_ref[...], sin_ref[...]
    lane = lax.broadcasted_iota(I32, cos.shape, 1)
    first_half = (lane & 63) < 32

    def rope(y):
        rot = jnp.where(first_half, pltpu.roll(y, LANES - 32, 1), pltpu.roll(y, 32, 1))
        return y * cos + rot * sin

    def seg(name):
        c0, width = PROJ_COLS[name]
        return _dot_nt(h, w_ref[c0:c0 + width, :])

    def groups(y):
        return [y[:, LANES * j:LANES * (j + 1)] for j in range(y.shape[1] // LANES)]

    for j, y in enumerate(groups(seg("aq"))):
        qa_ref[:, LANES * j:LANES * (j + 1)] = (rope(y) * 0.125).astype(BF16)
    for j, y in enumerate(groups(seg("iq"))):
        qi_ref[:, LANES * j:LANES * (j + 1)] = (rope(y) * 0.125).astype(BF16)
    for j, y in enumerate(groups(seg("bq"))):
        bq_ref[:, LANES * j:LANES * (j + 1)] = (rope(y) * 0.125).astype(BF16)
    ka = rope(seg("ak"))
    va = seg("av")
    y = seg("ik")
    ik = jnp.where(lane < 64, rope(y), y)
    bk = [rope(y) for y in groups(seg("bk"))]
    bv = groups(seg("bv"))
    if not prompt:
        ka32_ref[...] = ka
        va32_ref[...] = va
        ik32_ref[...] = ik
        for j in range(B_HEADS):
            bk32_ref[:, LANES * j:LANES * (j + 1)] = bk[j]
            bv32_ref[:, LANES * j:LANES * (j + 1)] = bv[j]
        return
    tm = ka.shape[0]
    kat32_ref[...] = ka.T
    ka16_ref[...] = ka.astype(BF16)
    vat = va.T
    vat32_ref[...] = vat
    vat16_ref[...] = vat.astype(BF16)
    ikt32_ref[...] = ik.T
    ik16_ref[...] = ik.astype(BF16)
    for j in range(B_HEADS):
        rows = pl.ds(j, tm, stride=B_HEADS)
        bk32_ref[rows, :] = bk[j]
        bv32_ref[rows, :] = bv[j]
        bk16_ref[:, LANES * j:LANES * (j + 1)] = bk[j].astype(BF16)
        bvt16_ref[LANES * j:LANES * (j + 1), :] = bv[j].T.astype(BF16)


def _proj(x, g, w, cos, sin, *, prompt, n_batch=1):
    rows, d = x.shape
    seq = cos.shape[0]
    tm = _row_tile(seq)
    n_t = seq // tm
    row = lambda width: pl.BlockSpec((tm, width), lambda i: (i, 0))
    tab = pl.BlockSpec((tm, LANES), lambda i: (i % n_t, 0))
    rm = lambda width, dt: (row(width), jax.ShapeDtypeStruct((rows, width), dt))
    outs = dict(qa=rm(1024, BF16), qi=rm(512, BF16), bq=rm(1024, BF16))
    if prompt:
        fm = lambda feat, dt: (pl.BlockSpec((None, feat, tm), lambda i: (i // n_t, 0, i % n_t)),
                               jax.ShapeDtypeStruct((n_batch, feat, seq), dt))
        tiled = lambda feat, dt: (pl.BlockSpec((None, feat, tm), lambda i: (i, 0, 0)),
                                  jax.ShapeDtypeStruct((rows // tm, feat, tm), dt))
        head_rows = (pl.BlockSpec((tm * B_HEADS, LANES), lambda i: (i, 0)),
                     jax.ShapeDtypeStruct((rows * B_HEADS, LANES), F32))
        outs.update(kat32=fm(LANES, F32), vat32=fm(LANES, F32), ikt32=fm(LANES, F32), bk32=head_rows,
                    bv32=head_rows, ka16=rm(LANES, BF16), ik16=rm(LANES, BF16), bk16=rm(512, BF16),
                    vat16=tiled(LANES, BF16), bvt16=tiled(512, BF16))
    else:
        outs.update(ka32=rm(LANES, F32), va32=rm(LANES, F32), ik32=rm(LANES, F32), bk32=rm(512, F32),
                    bv32=rm(512, F32))
    out = pl.pallas_call(
        functools.partial(_proj_body, prompt=prompt), grid=(rows // tm,),
        in_specs=[row(d), _resident((1, d)), _resident(w.shape), tab, tab],
        out_specs=[spec for spec, _ in outs.values()],
        out_shape=[shape for _, shape in outs.values()],
        compiler_params=_cparams(("arbitrary",)), name="proj_prompt" if prompt else "proj_sample",
    )(x, g.reshape(1, d), w, cos, sin)
    return dict(zip(outs.keys(), out))


def _memkv_body(x_ref, g_ref, w_ref, k32_ref, v32_ref, k16_ref, v16_ref):
    h = _rms(x_ref[...], g_ref[...]).astype(BF16)
    y = _dot(h, w_ref[...])
    half = y.shape[1] // 2
    k, v = y[:, :half], y[:, half:]
    k32_ref[...] = k
    v32_ref[...] = v
    k16_ref[...] = k.astype(BF16)
    v16_ref[...] = v.astype(BF16)


def _memkv(x, g, w):
    rows, d = x.shape
    half = w.shape[1] // 2
    tm = _row_tile(rows)
    row = lambda width: pl.BlockSpec((tm, width), lambda i: (i, 0))
    return pl.pallas_call(
        _memkv_body, grid=(rows // tm,),
        in_specs=[row(d), _resident((1, d)), _resident(w.shape)],
        out_specs=[row(half)] * 4,
        out_shape=[jax.ShapeDtypeStruct((rows, half), dt) for dt in (F32, F32, BF16, BF16)],
        compiler_params=_cparams(("arbitrary",)), name="memkv",
    )(x, g.reshape(1, d), w)


def _float_key(s):
    bits = pltpu.bitcast(s + 0.0, I32)
    return bits ^ ((bits >> 31) & 0x7FFFFFFF)


def _kth_largest_key(count_ge, rows_shape, k):
    base = jnp.where(count_ge(jnp.zeros(rows_shape, I32)) >= k, 0, INT_MIN).astype(I32)

    def bit_step(i, base):
        cand = base | (jnp.int32(1) << (30 - i))
        return jnp.where(count_ge(cand) >= k, cand, base)

    return lax.fori_loop(0, 31, bit_step, base)


I16 = jnp.int16


def _pack16(x):
    half = x.shape[0] // 2
    return (x[:half] & 0xFFFF) | (x[half:] << 16)


def _kth_largest_16(count_ge, rows_shape, k):
    word = lambda c: (c & 0xFFFF) | (c << 16)
    base = jnp.where(count_ge(jnp.zeros(rows_shape, I32)) >= k, 0, -(2 ** 15)).astype(I32)

    def bit_step(i, base):
        cand = base | (jnp.int32(1) << (14 - i))
        return jnp.where(count_ge(word(cand)) >= k, cand, base)

    return lax.fori_loop(0, 15, bit_step, base)


def _softmax_step(s, m_prev, l_prev, axis=-1):
    m_new = jnp.maximum(m_prev, jnp.max(s, axis=axis, keepdims=True))
    alpha = jnp.exp(m_prev - m_new)
    p = jnp.exp(s - m_new)
    l_new = alpha * l_prev + jnp.sum(p, axis=axis, keepdims=True)
    return p, m_new, l_new, alpha


def _dsa_body(qa_ref, qi_ref, ikt_ref, ka_ref, ik_ref, vat_ref, tril_ref, o_ref, key_scr, half_scr, acc_scr,
              *, tq, tk, top_k):
    q0 = pl.program_id(1) * tq
    n_kb = (q0 + tq + tk - 1) // tk
    rep = A_HEADS // A_KV_HEADS
    q_pos = q0 + lax.broadcasted_iota(I32, (1, tq), 1)
    k_iota = lax.broadcasted_iota(I32, (tk, 1), 0)

    qi = jnp.concatenate([qi_ref[:, LANES * h:LANES * (h + 1)] for h in range(IDX_HEADS)], axis=0)
    w_heads = [ikt_ref[IDX_DIM + h:IDX_DIM + h + 1, :] for h in range(IDX_HEADS)]

    def score_block(kb, carry):
        k0 = pl.multiple_of(kb * tk, tk)
        logits = _dot_nt(ik_ref[pl.ds(k0, tk), :], qi)
        s = jnp.zeros((tk, tq), F32)
        for h in range(IDX_HEADS):
            s = s + jnp.maximum(logits[:, h * tq:(h + 1) * tq], 0.0) * w_heads[h]
        s = jnp.where(k0 + k_iota <= q_pos, s, -jnp.inf)
        key = _float_key(s)
        key_scr[kb] = key
        half_scr[kb] = _pack16(key >> 16)
        return carry

    lax.fori_loop(0, n_kb, score_block, 0)

    def count_ge(cand):
        def blk(kb, acc):
            c = jnp.where(key_scr[kb] >= cand, 1, 0)
            return acc + jnp.sum(c.reshape(tk // 8, 8, tq), axis=0)
        acc = lax.fori_loop(0, n_kb, blk, jnp.zeros((8, tq), I32))
        return jnp.sum(acc, axis=0, keepdims=True)

    def count_ge_16(cand_word):
        cand = pltpu.bitcast(jnp.broadcast_to(cand_word, (8, tq)), I16)
        def blk(kb, acc):
            v = pltpu.bitcast(half_scr[kb], I16)
            ones = [jnp.where(v[16 * i:16 * (i + 1)] >= cand, jnp.int16(1), jnp.int16(0)) for i in range(tk // 16)]
            while len(ones) > 1:
                ones = [a + b for a, b in zip(ones[0::2], ones[1::2])]
            return acc + ones[0]
        acc = pltpu.bitcast(lax.fori_loop(0, n_kb, blk, jnp.zeros((16, tq), I16)), I32)
        return jnp.sum((acc & 0xFFFF) + (acc >> 16), axis=0, keepdims=True)

    thr_hi = _kth_largest_16(count_ge_16, (1, tq), top_k)

    def low_halves(kb, carry):
        key = key_scr[kb]
        hi = key >> 16
        low = (key & 0xFFFF) - 2 ** 15
        half_scr[kb] = _pack16(jnp.where(hi == thr_hi, low, jnp.where(hi > thr_hi, 2 ** 15 - 1, -(2 ** 15))))
        return carry

    lax.fori_loop(0, n_kb, low_halves, 0)
    thr_lo = _kth_largest_16(count_ge_16, (1, tq), top_k)
    thr = (thr_hi << 16) | ((thr_lo + 2 ** 15) & 0xFFFF)
    need = (top_k - count_ge(thr + 1)).astype(F32)
    thr_tie = jnp.where(thr == KEY_NEG_INF, jnp.int32(2 ** 31 - 1), thr)

    qa = jnp.concatenate([qa_ref[:, LANES * h:LANES * (h + 1)] for h in range(A_HEADS)], axis=0)
    acc_scr[...] = jnp.zeros(acc_scr.shape, F32)
    group_cols = [slice(g * rep * tq, (g + 1) * rep * tq) for g in range(A_KV_HEADS)]

    def attend_block(kb, carry):
        m, l, tie_seen = carry
        k0 = pl.multiple_of(kb * tk, tk)
        key = key_scr[kb]
        tie = key == thr_tie
        rank = tie_seen + _dot(tril_ref[...], jnp.where(tie, 1.0, 0.0).astype(BF16))
        bias = jnp.where(key > thr, 0.0, jnp.where(tie, jnp.where(rank <= need, 0.0, NEG_BIG), NEG_BIG))
        s = _dot_nt(ka_ref[pl.ds(k0, tk), :], qa)
        s = jnp.concatenate([s[:, h * tq:(h + 1) * tq] + bias for h in range(A_HEADS)], axis=1)
        p, m, l, alpha = _softmax_step(s, m, l, axis=0)
        p = p.astype(BF16)
        vt = vat_ref[kb]
        for g in range(A_KV_HEADS):
            v_g = vt[A_HEAD_DIM * g:A_HEAD_DIM * (g + 1), :]
            acc_scr[g] = alpha[:, group_cols[g]] * acc_scr[g] + _dot(v_g, p[:, group_cols[g]])
        return m, l, rank[tk - 1:tk, :]

    stats = (1, A_HEADS * tq)
    _, l, _ = lax.fori_loop(0, n_kb, attend_block,
                            (jnp.full(stats, NEG_BIG, F32), jnp.zeros(stats, F32), jnp.zeros((1, tq), F32)))

    for g in range(A_KV_HEADS):
        o = acc_scr[g] / l[:, group_cols[g]]
        for j in range(rep // 2):
            pair = jnp.concatenate([o[:, (2 * j) * tq:(2 * j + 1) * tq], o[:, (2 * j + 1) * tq:(2 * j + 2) * tq]], axis=0)
            c = g * (rep // 2) + j
            o_ref[:, LANES * c:LANES * (c + 1)] = pair.T.astype(BF16)


def _dsa_prompt(p, n_batch, seq, top_k):
    tq = min(128, seq)
    tk = p["vat16"].shape[2]
    rep = A_HEADS // A_KV_HEADS
    n_q = seq // tq
    tril = jnp.tril(jnp.ones((tk, tk), BF16))
    qrow = lambda width: pl.BlockSpec((tq, width), lambda b, i: (b * n_q + i, 0))
    seqblk = lambda width: pl.BlockSpec((seq, width), lambda b, i: (b, 0))
    return pl.pallas_call(
        functools.partial(_dsa_body, tq=tq, tk=tk, top_k=top_k),
        grid=(n_batch, n_q),
        in_specs=[qrow(1024), qrow(512), pl.BlockSpec((None, LANES, tq), lambda b, i: (b, 0, i)),
                  seqblk(LANES), seqblk(LANES), pl.BlockSpec((seq // tk, LANES, tk), lambda b, i: (b, 0, 0)),
                  _resident((tk, tk))],
        out_specs=qrow(512),
        out_shape=jax.ShapeDtypeStruct((n_batch * seq, 512), BF16),
        scratch_shapes=[pltpu.VMEM((seq // tk, tk, tq), I32),
                        pltpu.VMEM((seq // tk, tk // 2, tq), I32),
                        pltpu.VMEM((A_KV_HEADS, A_HEAD_DIM, rep * tq), F32)],
        compiler_params=_cparams(("arbitrary", "arbitrary")), name="dsa_prompt",
    )(p["qa"], p["qi"], p["ikt32"], p["ka16"], p["ik16"], p["vat16"], tril)


def _diff_lambda(lq1, lk1, lq2, lk2, lambda_init):
    s1 = jnp.sum(lq1 * lk1, axis=-1, keepdims=True)
    s2 = jnp.sum(lq2 * lk2, axis=-1, keepdims=True)
    return jnp.exp(s1) - jnp.exp(s2) + lambda_init


def _subln(o, gain, lambda_init):
    return _rms(o, gain) * (1.0 - lambda_init)


def _diff_body(bq_ref, bk_ref, bvt_ref, lq1_ref, lk1_ref, lq2_ref, lk2_ref, gain_ref, o_ref, acc_scr,
               *, tq, tk, lambda_init):
    q0 = pl.program_id(1) * tq
    n_full = q0 // tk
    qp = q0 + lax.broadcasted_iota(I32, (1, tq), 1)
    q_pos = jnp.concatenate([qp, qp], axis=1)
    k_iota = lax.broadcasted_iota(I32, (tk, 1), 0)
    acc_scr[...] = jnp.zeros(acc_scr.shape, F32)
    q_heads = [jnp.concatenate([bq_ref[:, LANES * (2 * h + c):LANES * (2 * h + c + 1)] for c in range(2)], axis=0)
               for h in range(B_HEADS)]

    def block(kb, stats, masked):
        k0 = pl.multiple_of(kb * tk, tk)
        vt = bvt_ref[kb]
        out = []
        for h in range(B_HEADS):
            m, l = stats[2 * h], stats[2 * h + 1]
            s = _dot_nt(bk_ref[pl.ds(k0, tk), LANES * h:LANES * (h + 1)], q_heads[h])
            if masked:
                s = jnp.where(k0 + k_iota <= q_pos, s, NEG_BIG)
            p, m, l, alpha = _softmax_step(s, m, l, axis=0)
            acc_scr[h] = alpha * acc_scr[h] + _dot(vt[LANES * h:LANES * (h + 1), :], p.astype(BF16))
            out += [m, l]
        return tuple(out)

    shape = (1, 2 * tq)
    init = tuple(jnp.full(shape, NEG_BIG, F32) if i % 2 == 0 else jnp.zeros(shape, F32) for i in range(2 * B_HEADS))
    stats = lax.fori_loop(0, n_full, lambda kb, st: block(kb, st, False), init)
    for d in range(-(-tq // tk)):
        stats = block(n_full + d, stats, True)

    lam = _diff_lambda(lq1_ref[...], lk1_ref[...], lq2_ref[...], lk2_ref[...], lambda_init)
    for h in range(B_HEADS):
        o = acc_scr[h] / stats[2 * h + 1]
        o = (o[:, :tq] - lam * o[:, tq:]).T
        o_ref[:, LANES * h:LANES * (h + 1)] = _subln(o, gain_ref[...], lambda_init).astype(BF16)


def _diff_prompt(p, lam_params, gain, n_batch, seq, lambda_init):
    tk = p["bvt16"].shape[2]
    tq = min(256, seq)
    assert tk % tq == 0 or tq % tk == 0
    n_q = seq // tq
    qrow = lambda width: pl.BlockSpec((tq, width), lambda b, i: (b * n_q + i, 0))
    small = [_resident((1, a.shape[-1])) for a in lam_params] + [_resident((1, B_V_DIM))]
    return pl.pallas_call(
        functools.partial(_diff_body, tq=tq, tk=tk, lambda_init=lambda_init),
        grid=(n_batch, n_q),
        in_specs=[qrow(1024), pl.BlockSpec((seq, 512), lambda b, i: (b, 0)),
                  pl.BlockSpec((seq // tk, 512, tk), lambda b, i: (b, 0, 0))] + small,
        out_specs=qrow(512),
        out_shape=jax.ShapeDtypeStruct((n_batch * seq, 512), BF16),
        scratch_shapes=[pltpu.VMEM((B_HEADS, B_V_DIM, 2 * tq), F32)],
        compiler_params=_cparams(("arbitrary", "arbitrary")), name="diff_prompt",
    )(p["bq"], p["bk16"], p["bvt16"], *[a.reshape(1, -1) for a in lam_params], gain.reshape(1, -1))


def _post_body(*refs, mem_in_kernel):
    if mem_in_kernel:
        (x_ref, g_ref, oa_ref, ob_ref, mk_ref, mv_ref, wmq_ref, wg_ref, wpa_ref, wpb_ref, wpm_ref, wo_ref,
         o_ref) = refs
    else:
        x_ref, g_ref, oa_ref, ob_ref, om_ref, wg_ref, wpa_ref, wpb_ref, wpm_ref, wo_ref, o_ref = refs
    x = x_ref[...]
    d = x.shape[1]
    h = _rms(x, g_ref[...]).astype(BF16)
    if mem_in_kernel:
        mq = (_dot_nt(h, wmq_ref[...]) * (M_HEAD_DIM ** -0.5)).astype(BF16)
        heads = []
        for hd in range(M_HEADS):
            sl = slice(LANES * hd, LANES * (hd + 1))
            s = _dot_nt(mq[:, sl], mk_ref[:, sl])
            e = jnp.exp(s - jnp.max(s, axis=-1, keepdims=True))
            o = _dot(e.astype(BF16), mv_ref[:, sl]) / jnp.sum(e, axis=-1, keepdims=True)
            heads.append(o.astype(BF16))
        om = jnp.concatenate(heads, axis=1)
    else:
        om = om_ref[...]

    def gate(i):
        return jax.nn.sigmoid(_dot_nt(h, wg_ref[d * i:d * (i + 1), :]))

    merged = gate(0) * _dot(oa_ref[...], wpa_ref[...])
    merged = merged + gate(1) * _dot(ob_ref[...], wpb_ref[...])
    merged = merged + gate(2) * _dot(om, wpm_ref[...])
    o_ref[...] = x + _dot(merged.astype(BF16), wo_ref[...])


def _post(x, g, oa, ob, w, *, mem=None, om=None, n_batch=1):
    rows, d = x.shape
    seq = rows // n_batch
    tm = _row_tile(seq)
    n_t = seq // tm
    row = lambda width: pl.BlockSpec((tm, width), lambda b, i: (b * n_t + i, 0))
    weights = [w["gates"], w["proj_a"], w["proj_b"], w["proj_m"], w["out"]]
    if mem is not None:
        mk, mv, w_mq = mem
        memblk = pl.BlockSpec((MEM_TOKENS, mk.shape[1]), lambda b, i: (b, 0))
        args = [x, g.reshape(1, d), oa, ob, mk, mv, w_mq] + weights
        in_specs = ([row(d), _resident((1, d)), row(512), row(512), memblk, memblk, _resident(w_mq.shape)]
                    + [_resident(a.shape) for a in weights])
    else:
        args = [x, g.reshape(1, d), oa, ob, om] + weights
        in_specs = [row(d), _resident((1, d)), row(512), row(512), row(512)] + [_resident(a.shape) for a in weights]
    return pl.pallas_call(
        functools.partial(_post_body, mem_in_kernel=mem is not None),
        grid=(n_batch, n_t), in_specs=in_specs, out_specs=row(d),
        out_shape=jax.ShapeDtypeStruct((rows, d), F32),
        compiler_params=_cparams(("arbitrary", "arbitrary")), name="post",
    )(*args)


def _mem_sample_body(x_ref, g_ref, wmq_ref, mk_ref, mv_ref, o_ref, mq_scr):
    b = pl.program_id(0)

    @pl.when(b == 0)
    def _():
        h = _rms(x_ref[...], g_ref[...]).astype(BF16)
        mq_scr[...] = _dot_nt(h, wmq_ref[...]) * (M_HEAD_DIM ** -0.5)

    q = mq_scr[pl.ds(b, 1), :]
    for hd in range(M_HEADS):
        sl = slice(LANES * hd, LANES * (hd + 1))
        rows = pl.ds(hd, MEM_TOKENS, stride=M_HEADS)
        s = jnp.sum(mk_ref[rows, :] * q[:, sl], axis=-1, keepdims=True)
        e = jnp.exp(s - jnp.max(s, axis=0, keepdims=True))
        o = jnp.sum(e * mv_ref[rows, :], axis=0, keepdims=True) / jnp.sum(e, axis=0, keepdims=True)
        o_ref[:, sl] = o.astype(BF16)


def _mem_sample(x, g, w_mq, cache_k, cache_v, layer):
    n, d = x.shape
    width = w_mq.shape[0]
    memblk = pl.BlockSpec((None, None) + cache_k.shape[2:], lambda b: (b, layer, 0, 0))
    return pl.pallas_call(
        _mem_sample_body, grid=(n,),
        in_specs=[_resident((n, d)), _resident((1, d)), _resident(w_mq.shape), memblk, memblk],
        out_specs=pl.BlockSpec((None, 1, width), lambda b: (b, 0, 0)),
        out_shape=jax.ShapeDtypeStruct((n, 1, width), BF16),
        scratch_shapes=[pltpu.VMEM((n, width), F32)],
        compiler_params=_cparams(("arbitrary",)), name="mem_sample",
    )(x, g.reshape(1, d), w_mq, cache_k, cache_v)


def _page_copies(pt_ref, b, chunk, pages_per_chunk, layer, streams):
    slot = chunk % 2
    out = []
    for cache, dst, sem in streams:
        for j in range(pages_per_chunk):
            page = pt_ref[b, chunk * pages_per_chunk + j]
            out.append(pltpu.make_async_copy(cache.at[page, layer], dst(slot, j), sem.at[slot]))
    return out


def _lane_pages(buf, page):
    return lambda slot, j: buf.at[slot, :, pl.ds(page * j, page)]


def _row_pages(buf, rows):
    return lambda slot, j: buf.at[slot, pl.ds(rows * j, rows), :]


def _chunks(n_pages):
    ppc = min(16, n_pages)
    assert n_pages % ppc == 0
    return ppc, n_pages // ppc


def _for_each_chunk(pt_ref, n_pages, layer, streams, body):
    b = pl.program_id(0)
    n_samples = pl.num_programs(0)
    ppc, n_ch = _chunks(n_pages)
    across_samples = n_ch % 2 == 0

    def start(sample, c):
        for cp in _page_copies(pt_ref, sample, c, ppc, layer, streams):
            cp.start()

    if across_samples:
        pl.when(b == 0)(lambda: start(b, 0))
    else:
        start(b, 0)
    for c in range(n_ch):
        if c + 1 < n_ch:
            start(b, c + 1)
        elif across_samples:
            pl.when(b + 1 < n_samples)(lambda: start(b + 1, 0))
        for cp in _page_copies(pt_ref, b, c, ppc, layer, streams):
            cp.wait()
        body(c)


def _idx_sample_body(pt_ref, qi_ref, w_ref, ikn_ref, cache_ref, s_ref, buf, sem, *, layer, n_pages, page):
    ppc, n_ch = _chunks(n_pages)
    ck = ppc * page
    q = qi_ref[:, 0:IDX_DIM]
    w = w_ref[...]

    def scores(logits):
        return jnp.sum(jnp.maximum(logits, 0.0) * w, axis=0, keepdims=True)

    def chunk(c):
        s_ref[:, ck * c:ck * (c + 1)] = scores(_dot(q, buf[c % 2].astype(BF16)))

    _for_each_chunk(pt_ref, n_pages, layer, [(cache_ref, _lane_pages(buf, page), sem)], chunk)
    lane = lax.broadcasted_iota(I32, (1, LANES), 1)
    new_key = ikn_ref[:, 0:IDX_DIM].astype(BF16).astype(F32)
    s_new = scores(jnp.sum(q.astype(F32) * new_key, axis=1, keepdims=True))
    s_ref[:, ck * n_ch:] = jnp.where(lane == 0, s_new, -jnp.inf)


def _idx_sample(page_table, qi8, w8, ik_new, cache_t, layer):
    n, n_pages = page_table.shape
    feat, page = cache_t.shape[2], cache_t.shape[3]
    ppc, n_ch = _chunks(n_pages)
    nk = n_pages * page + LANES
    grid_spec = pltpu.PrefetchScalarGridSpec(
        num_scalar_prefetch=1, grid=(n,),
        in_specs=[pl.BlockSpec((None, 8, LANES), lambda b, pt: (b, 0, 0)),
                  pl.BlockSpec((None, 8, 1), lambda b, pt: (b, 0, 0)),
                  pl.BlockSpec((None, 1, LANES), lambda b, pt: (b, 0, 0)),
                  pl.BlockSpec(memory_space=pl.ANY)],
        out_specs=pl.BlockSpec((None, 1, nk), lambda b, pt: (b, 0, 0)),
        scratch_shapes=[pltpu.VMEM((2, feat, ppc * page), F32), pltpu.SemaphoreType.DMA((2,))])
    return pl.pallas_call(
        functools.partial(_idx_sample_body, layer=layer, n_pages=n_pages, page=page),
        grid_spec=grid_spec, out_shape=jax.ShapeDtypeStruct((n, 1, nk), F32),
        compiler_params=_cparams(("arbitrary",)), name="idx_sample",
    )(page_table, qi8, w8, ik_new, cache_t)


def _select_body(s_ref, bias_ref, *, top_k, n_valid):
    key = _float_key(s_ref[...])
    rows, nk = key.shape
    shape = (rows, 1)

    def count(ones):
        return jnp.sum(ones, axis=1, keepdims=True)

    thr = _kth_largest_key(lambda cand: count(jnp.where(key >= cand, 1, 0)), shape, top_k)
    need = top_k - count(jnp.where(key > thr, 1, 0))
    thr_tie = jnp.where(thr == KEY_NEG_INF, jnp.int32(2 ** 31 - 1), thr)
    tie = jnp.where(key == thr_tie, 1, 0)
    idx = lax.broadcasted_iota(I32, key.shape, 1)
    n_bits = max(1, (nk - 1).bit_length())

    def bit_step(i, last):
        cand = last - (jnp.int32(1) << (n_bits - 1 - i))
        return jnp.where(count(jnp.where(idx <= cand, tie, 0)) >= need, cand, last)

    last = lax.fori_loop(0, n_bits, bit_step, jnp.full(shape, 2 ** n_bits - 1, I32))
    tie_bias = jnp.where(idx <= last, jnp.where(tie > 0, 0.0, NEG_BIG), NEG_BIG)
    bias = jnp.where(key > thr, 0.0, tie_bias)
    bias_ref[...] = jnp.where(idx < n_valid, bias, NEG_BIG)


def _select(scores, top_k, n_valid):
    return pl.pallas_call(
        functools.partial(_select_body, top_k=top_k, n_valid=n_valid),
        out_shape=jax.ShapeDtypeStruct(scores.shape, F32),
        compiler_params=pltpu.CompilerParams(vmem_limit_bytes=V7X_VMEM_LIMIT_BYTES), name="select_sample",
    )(scores)


def _bf16_round(x):
    return x.astype(BF16).astype(F32)


def _new_row_step(s_new, v_new, m, l, acc):
    m_new = jnp.maximum(m, s_new)
    alpha = jnp.exp(m - m_new)
    p_new = jnp.exp(s_new - m_new)
    return alpha * l + p_new, alpha * acc + p_new * v_new


def _dsa_sample_body(pt_ref, q_ref, kn_ref, vn_ref, bias_ref, kc_ref, vc_ref, o_ref, kbuf, vbuf, ksem, vsem,
                     *, layer, n_pages, page):
    ppc, n_ch = _chunks(n_pages)
    ck = ppc * page
    q = q_ref[...]
    st = dict(m=jnp.full((8, 1), NEG_BIG, F32), l=jnp.zeros((8, 1), F32), acc=jnp.zeros((8, LANES), F32))

    def chunk(c):
        s = _dot(q, kbuf[c % 2].astype(BF16)) + bias_ref[:, ck * c:ck * (c + 1)]
        p, st["m"], st["l"], alpha = _softmax_step(s, st["m"], st["l"])
        st["acc"] = alpha * st["acc"] + _dot_nt(p.astype(BF16), vbuf[c % 2].astype(BF16))

    _for_each_chunk(pt_ref, n_pages, layer,
                    [(kc_ref, _lane_pages(kbuf, page), ksem), (vc_ref, _lane_pages(vbuf, page), vsem)], chunk)
    s_new = (jnp.sum(q.astype(F32) * _bf16_round(kn_ref[...]), axis=1, keepdims=True)
             + bias_ref[:, ck * n_ch:ck * n_ch + 1])
    l, acc = _new_row_step(s_new, _bf16_round(vn_ref[...]), st["m"], st["l"], st["acc"])
    o_ref[...] = acc / l


def _diff_sample_body(pt_ref, q_ref, kn_ref, vn_ref, kc_ref, vc_ref, lq1_ref, lk1_ref, lq2_ref, lk2_ref,
                      gain_ref, o_ref, kbuf, vbuf, ksem, vsem, *, layer, n_pages, page, lambda_init):
    ppc, n_ch = _chunks(n_pages)
    ck = ppc * page
    page_rows = page * B_HEADS
    q = q_ref[...]
    row_head = lax.broadcasted_iota(I32, (8, 1), 0) >> 1
    st = dict(m=jnp.full((8, 1), NEG_BIG, F32), l=jnp.zeros((8, 1), F32), acc=jnp.zeros((8, B_HEADS * LANES), F32))

    def chunk(c):
        head_rows = lambda buf, h: buf[c % 2, pl.ds(h, ck, stride=B_HEADS), :].astype(BF16)
        s = jnp.zeros((8, ck), F32)
        for h in range(B_HEADS):
            s = jnp.where(row_head == h, _dot_nt(q, head_rows(kbuf, h)), s)
        p, st["m"], st["l"], alpha = _softmax_step(s, st["m"], st["l"])
        pb = p.astype(BF16)
        pv = jnp.concatenate([_dot(pb, head_rows(vbuf, h)) for h in range(B_HEADS)], axis=1)
        st["acc"] = alpha * st["acc"] + pv

    _for_each_chunk(pt_ref, n_pages, layer,
                    [(kc_ref, _row_pages(kbuf, page_rows), ksem), (vc_ref, _row_pages(vbuf, page_rows), vsem)], chunk)
    qf = q.astype(F32)
    kn = _bf16_round(kn_ref[...])
    s_new = jnp.zeros((8, 1), F32)
    for h in range(B_HEADS):
        s_h = jnp.sum(qf * kn[:, LANES * h:LANES * (h + 1)], axis=1, keepdims=True)
        s_new = jnp.where(row_head == h, s_h, s_new)
    l, acc = _new_row_step(s_new, _bf16_round(vn_ref[...]), st["m"], st["l"], st["acc"])
    o = acc / l
    lam = _diff_lambda(lq1_ref[...], lk1_ref[...], lq2_ref[...], lk2_ref[...], lambda_init)
    for h in range(B_HEADS):
        sl = slice(LANES * h, LANES * (h + 1))
        oh = o[2 * h:2 * h + 1, sl] - lam * o[2 * h + 1:2 * h + 2, sl]
        o_ref[:, sl] = _subln(oh, gain_ref[...], lambda_init).astype(BF16)


def _per_sample(a):
    return pl.BlockSpec((None,) + a.shape[1:], lambda b, pt: (b,) + (0,) * (a.ndim - 1))


def _dsa_sample(page_table, q8, k_new, v_new, bias, cache_kt, cache_vt, layer):
    n, n_pages = page_table.shape
    feat, page = cache_kt.shape[2], cache_kt.shape[3]
    ppc, _ = _chunks(n_pages)
    any_spec = pl.BlockSpec(memory_space=pl.ANY)
    grid_spec = pltpu.PrefetchScalarGridSpec(
        num_scalar_prefetch=1, grid=(n,),
        in_specs=[_per_sample(q8), _per_sample(k_new), _per_sample(v_new), _per_sample(bias), any_spec, any_spec],
        out_specs=pl.BlockSpec((None, 8, feat), lambda b, pt: (b, 0, 0)),
        scratch_shapes=[pltpu.VMEM((2, feat, ppc * page), F32), pltpu.VMEM((2, feat, ppc * page), F32),
                        pltpu.SemaphoreType.DMA((2,)), pltpu.SemaphoreType.DMA((2,))])
    return pl.pallas_call(
        functools.partial(_dsa_sample_body, layer=layer, n_pages=n_pages, page=page),
        grid_spec=grid_spec, out_shape=jax.ShapeDtypeStruct((n, 8, feat), F32),
        compiler_params=_cparams(("arbitrary",)), name="dsa_sample",
    )(page_table, q8, k_new, v_new, bias, cache_kt, cache_vt)


def _diff_sample(page_table, q8, k_new, v_new, cache_k, cache_v, layer, diff_params, lambda_init):
    n, n_pages = page_table.shape
    page_rows = cache_k.shape[2]
    ppc, _ = _chunks(n_pages)
    width = B_HEADS * B_V_DIM
    any_spec = pl.BlockSpec(memory_space=pl.ANY)
    extra = [a.reshape(1, -1) for a in diff_params]
    grid_spec = pltpu.PrefetchScalarGridSpec(
        num_scalar_prefetch=1, grid=(n,),
        in_specs=([_per_sample(q8), _per_sample(k_new), _per_sample(v_new), any_spec, any_spec]
                  + [pl.BlockSpec(a.shape, lambda b, pt: (0, 0)) for a in extra]),
        out_specs=pl.BlockSpec((None, 1, width), lambda b, pt: (b, 0, 0)),
        scratch_shapes=[pltpu.VMEM((2, ppc * page_rows, LANES), F32), pltpu.VMEM((2, ppc * page_rows, LANES), F32),
                        pltpu.SemaphoreType.DMA((2,)), pltpu.SemaphoreType.DMA((2,))])
    return pl.pallas_call(
        functools.partial(_diff_sample_body, layer=layer, n_pages=n_pages, page=page_rows // B_HEADS,
                          lambda_init=lambda_init),
        grid_spec=grid_spec, out_shape=jax.ShapeDtypeStruct((n, 1, width), BF16),
        compiler_params=_cparams(("arbitrary",)), name="diff_sample",
    )(page_table, q8, k_new, v_new, cache_k, cache_v, *extra)


def _layer_weights(l, w_in_t, w_proj_a, w_proj_b, w_proj_m, w_out):
    o_mq = sum((512, 128, 128, 256, 64, 4, 512, 512, 512))
    wt = w_in_t[:, l, :]
    return dict(proj=_proj_weight(wt),
                mq=wt[o_mq:o_mq + 512].astype(BF16),
                gates=wt[o_mq + 512:].astype(BF16),
                proj_a=w_proj_a[l].astype(BF16), proj_b=w_proj_b[l].astype(BF16),
                proj_m=w_proj_m[l].astype(BF16), out=w_out[l].astype(BF16))


def kernel(x_prompt, x_sample, cache_a_k, cache_a_v, cache_idx_k, cache_b_k, cache_b_v, cache_mem_k, cache_mem_v, page_table, mem_prompt, norm_ffn1, ffn1_w_in, ffn1_w_out, norm_mix, w_in, lambda_q1, lambda_k1, lambda_q2, lambda_k2, subln_gain, norm_mem, w_mem_kv, w_proj_a, w_proj_b, w_proj_m, w_out, norm_ffn2, ffn2_w_in, ffn2_w_out, final_norm):
    n_b, seq, d = x_prompt.shape
    n_s, dec_seq, _ = x_sample.shape
    assert dec_seq == 1
    depth = norm_ffn1.shape[0]
    n_pool, _, page = cache_a_k.shape[:3]
    n_pages = page_table.shape[1]
    past_len = n_pages * page
    top_k_prompt = min(IDX_TOPK_MAX, seq // 4)
    top_k_sample = min(IDX_TOPK_MAX, (past_len + dec_seq) // 4)

    cos_p, sin_p = _rope_tables(jnp.arange(seq, dtype=I32))
    cos_s, sin_s = _rope_tables(jnp.full((n_s,), past_len, I32))
    c_ik = jnp.transpose(cache_idx_k, (0, 1, 3, 2))
    c_ak = jnp.transpose(cache_a_k, (0, 1, 3, 4, 2)).reshape(n_pool, depth, -1, page)
    c_av = jnp.transpose(cache_a_v, (0, 1, 3, 4, 2)).reshape(n_pool, depth, -1, page)
    c_bk = cache_b_k.reshape(n_pool, depth, page * B_HEADS, -1)
    c_bv = cache_b_v.reshape(n_pool, depth, page * B_HEADS, -1)
    c_mk = cache_mem_k.reshape(n_s, depth, MEM_TOKENS * M_HEADS, -1)
    c_mv = cache_mem_v.reshape(n_s, depth, MEM_TOKENS * M_HEADS, -1)
    mem_rows = mem_prompt.reshape(n_b * MEM_TOKENS, d)
    w_in_t = jnp.transpose(w_in, (2, 0, 1))

    xp = x_prompt.reshape(n_b * seq, d)
    xs = x_sample.reshape(n_s, d)
    rows_p, rows_s, mem_rows_out = [], [], []
    yp = ys = None
    for l in range(depth):
        last = l == depth - 1
        lambda_init = 0.8 - 0.6 * math.exp(-0.3 * l)
        w = _layer_weights(l, w_in_t, w_proj_a, w_proj_b, w_proj_m, w_out)
        f1_in, f1_out = ffn1_w_in[l].astype(BF16), ffn1_w_out[l].astype(BF16)
        f2_in, f2_out = ffn2_w_in[l].astype(BF16), ffn2_w_out[l].astype(BF16)
        lam_params = (lambda_q1[l], lambda_k1[l], lambda_q2[l], lambda_k2[l])

        mk32, mv32, mk16, mv16 = _memkv(mem_rows, norm_mem[l], w_mem_kv[l].astype(BF16))
        mem_rows_out.append((mk32, mv32))
        xp = _ffn(xp, norm_ffn1[l], f1_in, f1_out)
        p = _proj(xp, norm_mix[l], w["proj"], cos_p, sin_p, prompt=True, n_batch=n_b)
        rows_p.append(p)
        oa = _dsa_prompt(p, n_b, seq, top_k_prompt)
        ob = _diff_prompt(p, lam_params, subln_gain[l], n_b, seq, lambda_init)
        xp = _post(xp, norm_mix[l], oa, ob, w, mem=(mk16, mv16, w["mq"]), n_batch=n_b)
        if last:
            xp, yp = _ffn(xp, norm_ffn2[l], f2_in, f2_out, final_norm)
        else:
            xp = _ffn(xp, norm_ffn2[l], f2_in, f2_out)

        xs = _ffn(xs, norm_ffn1[l], f1_in, f1_out)
        s = _proj(xs, norm_mix[l], w["proj"], cos_s, sin_s, prompt=False)
        rows_s.append(s)
        qi8 = jnp.pad(s["qi"].reshape(n_s, IDX_HEADS, LANES), ((0, 0), (0, 8 - IDX_HEADS), (0, 0)))
        w8 = jnp.pad(s["ik32"][:, IDX_DIM:IDX_DIM + IDX_HEADS], ((0, 0), (0, 8 - IDX_HEADS)))[:, :, None]
        scores = _idx_sample(page_table, qi8, w8, s["ik32"].reshape(n_s, 1, LANES), c_ik, l)
        bias = _select(scores.reshape(n_s, -1), top_k_sample, past_len + 1).reshape(n_s, 1, -1)
        oa8 = _dsa_sample(page_table, s["qa"].reshape(n_s, A_HEADS, LANES), s["ka32"].reshape(n_s, 1, -1),
                          s["va32"].reshape(n_s, 1, -1), bias, c_ak, c_av, l)
        rep = A_HEADS // A_KV_HEADS
        oa_s = jnp.concatenate([oa8[:, h, 64 * (h // rep):64 * (h // rep) + 64] for h in range(A_HEADS)],
                               axis=1).astype(BF16)
        ob_s = _diff_sample(page_table, s["bq"].reshape(n_s, 2 * B_HEADS, LANES), s["bk32"].reshape(n_s, 1, -1),
                            s["bv32"].reshape(n_s, 1, -1), c_bk, c_bv, l, lam_params + (subln_gain[l],),
                            lambda_init).reshape(n_s, -1)
        om_s = _mem_sample(xs, norm_mix[l], w["mq"], c_mk, c_mv, l).reshape(n_s, -1)
        xs = _post(xs, norm_mix[l], oa_s, ob_s, w, om=om_s)
        if last:
            xs, ys = _ffn(xs, norm_ffn2[l], f2_in, f2_out, final_norm)
        else:
            xs = _ffn(xs, norm_ffn2[l], f2_in, f2_out)

    def stack(rows, name, n, t, shape):
        width = int(math.prod(shape))
        return jnp.stack([r[name][:, :width].reshape(n, t, *shape) for r in rows], axis=1)

    def stack_fm(name, feat, shape):
        a = jnp.stack([r[name][:, :feat, :] for r in rows_p], axis=1)
        return jnp.moveaxis(a.reshape(n_b, depth, *shape, seq), -1, 2)

    def stack_mem(i):
        return jnp.stack([r[i].reshape(n_b, MEM_TOKENS, M_HEADS, M_HEAD_DIM) for r in mem_rows_out], axis=1)

    kv_p = (stack_fm("kat32", LANES, (A_KV_HEADS, A_HEAD_DIM)), stack_fm("vat32", LANES, (A_KV_HEADS, A_HEAD_DIM)),
            stack_fm("ikt32", IDX_DIM, (IDX_DIM,)),
            stack(rows_p, "bk32", n_b, seq, (B_HEADS, 2 * B_QK_DIM)), stack(rows_p, "bv32", n_b, seq, (B_HEADS, B_V_DIM)))
    kv_s = (stack(rows_s, "ka32", n_s, dec_seq, (A_KV_HEADS, A_HEAD_DIM)),
            stack(rows_s, "va32", n_s, dec_seq, (A_KV_HEADS, A_HEAD_DIM)),
            stack(rows_s, "ik32", n_s, dec_seq, (IDX_DIM,)),
            stack(rows_s, "bk32", n_s, dec_seq, (B_HEADS, 2 * B_QK_DIM)),
            stack(rows_s, "bv32", n_s, dec_seq, (B_HEADS, B_V_DIM)))
    return ((yp.reshape(n_b, seq, d), ys.reshape(n_s, dec_seq, d)) + kv_p + (stack_mem(0), stack_mem(1)) + kv_s)
```

```python
import functools
import math

import jax
import jax.numpy as jnp
from jax import lax
from jax.experimental import pallas as pl
from jax.experimental.pallas import tpu as pltpu

F32 = jnp.float32
BF16 = jnp.bfloat16
I32 = jnp.int32

A_HEADS, A_KV_HEADS, A_HEAD_DIM = 8, 2, 64
IDX_HEADS, IDX_DIM, IDX_TOPK_MAX = 4, 64, 256
B_HEADS, B_QK_DIM, B_V_DIM = 4, 64, 128
MEM_TOKENS, M_HEADS, M_HEAD_DIM = 256, 4, 128
N_BRANCH = 3
ROPE_THETA = 10000.0
RMS_EPS = 1e-6

LANES = 128
V7X_VMEM_LIMIT_BYTES = 60000 * 1024

NEG_BIG = -1e30
INT_MIN = -(2 ** 31)
KEY_NEG_INF = -(2 ** 31) + 0x7FFFFF

PROJ_COLS = dict(aq=(0, 1024), ak=(1024, 128), av=(1152, 128), ik=(1280, 128), iq=(1408, 512),
                 bq=(1920, 1024), bk=(2944, 512), bv=(3456, 512))
PROJ_WIDTH = 3968


def _cparams(sem):
    return pltpu.CompilerParams(dimension_semantics=sem, vmem_limit_bytes=V7X_VMEM_LIMIT_BYTES)


def _resident(shape):
    zeros = (0,) * len(shape)
    return pl.BlockSpec(shape, lambda *_: zeros, pipeline_mode=pl.Buffered(1))


def _rms(x, g):
    return x * lax.rsqrt(jnp.mean(x * x, axis=-1, keepdims=True) + RMS_EPS) * g


def _dot(a, b):
    return jnp.dot(a, b, preferred_element_type=F32)


def _dot_nt(a, b):
    return lax.dot_general(a, b, (((1,), (1,)), ((), ())), preferred_element_type=F32)


def _row_tile(rows, target=512):
    t = min(rows, target)
    assert rows % t == 0 and t % 8 == 0, (rows, t)
    return t


def _ffn_body(*refs, d_ff, chunk, final):
    if final:
        x_ref, g_ref, wi_ref, wo_ref, fg_ref, o_ref, of_ref = refs
    else:
        x_ref, g_ref, wi_ref, wo_ref, o_ref = refs
    x = x_ref[...]
    h = _rms(x, g_ref[...]).astype(BF16)
    acc = jnp.zeros_like(x)
    for c in range(d_ff // chunk):
        gate = _dot(h, wi_ref[:, c * chunk:(c + 1) * chunk])
        up = _dot(h, wi_ref[:, d_ff + c * chunk:d_ff + (c + 1) * chunk])
        act = (gate * jax.nn.sigmoid(gate) * up).astype(BF16)
        acc = acc + _dot(act, wo_ref[c * chunk:(c + 1) * chunk, :])
    y = x + 0.5 * acc
    o_ref[...] = y
    if final:
        of_ref[...] = _rms(y, fg_ref[...])


def _ffn(x, g, w_in, w_out, final_g=None):
    rows, d = x.shape
    d_ff = w_out.shape[0]
    chunk = d_ff // 2 if (d_ff // 2) % LANES == 0 else d_ff
    tm = _row_tile(rows)
    final = final_g is not None
    row_spec = pl.BlockSpec((tm, d), lambda i: (i, 0))
    in_specs = [row_spec, _resident((1, d)), _resident(w_in.shape), _resident(w_out.shape)]
    args = [x, g.reshape(1, d), w_in, w_out]
    out_shape = [jax.ShapeDtypeStruct((rows, d), F32)]
    out_specs = [row_spec]
    if final:
        in_specs.append(_resident((1, d)))
        args.append(final_g.reshape(1, d))
        out_shape.append(jax.ShapeDtypeStruct((rows, d), F32))
        out_specs.append(row_spec)
    out = pl.pallas_call(
        functools.partial(_ffn_body, d_ff=d_ff, chunk=chunk, final=final),
        grid=(rows // tm,), in_specs=in_specs, out_specs=out_specs, out_shape=out_shape,
        compiler_params=_cparams(("arbitrary",)), name="ffn_final" if final else "ffn",
    )(*args)
    return out if final else out[0]


def _rope_tables(pos):
    half = A_HEAD_DIM // 2
    inv = ROPE_THETA ** (-jnp.arange(half, dtype=F32) / half)
    ang = pos.astype(F32)[:, None] * inv[None, :]
    cos, sin = jnp.cos(ang), jnp.sin(ang)
    cos = jnp.tile(cos, (1, LANES // half))
    sin = jnp.tile(jnp.concatenate([-sin, sin], axis=1), (1, LANES // (2 * half)))
    return cos, sin


def _proj_weight(wt):
    d = wt.shape[1]
    z64 = jnp.zeros((64, d), wt.dtype)
    o = 0
    aq = wt[o:o + 512]; o += 512
    ak = wt[o:o + 128]; o += 128
    av = wt[o:o + 128]; o += 128
    iq = wt[o:o + 256]; o += 256
    ik = wt[o:o + 64]; o += 64
    iw = wt[o:o + 4]; o += 4
    bq = wt[o:o + 512]; o += 512
    bk = wt[o:o + 512]; o += 512
    bv = wt[o:o + 512]; o += 512
    rows = []
    for h in range(A_HEADS):
        qh = aq[64 * h:64 * h + 64]
        rows += [qh, z64] if h // (A_HEADS // A_KV_HEADS) == 0 else [z64, qh]
    rows += [ak, av, ik, iw, jnp.zeros((60, d), wt.dtype)]
    for h in range(IDX_HEADS):
        rows += [iq[64 * h:64 * h + 64], z64]
    for j in range(2 * B_HEADS):
        qj = bq[64 * j:64 * j + 64]
        rows += [qj, z64] if j % 2 == 0 else [z64, qj]
    rows += [bk, bv]
    w = jnp.concatenate(rows, axis=0)
    assert w.shape[0] == PROJ_WIDTH
    return w.astype(BF16)


def _proj_body(x_ref, g_ref, w_ref, cos_ref, sin_ref, qa_ref, qi_ref, bq_ref, *kv_refs, prompt):
    if prompt:
        kat32_ref, vat32_ref, ikt32_ref, bk32_ref, bv32_ref, ka16_ref, ik16_ref, bk16_ref, vat16_ref, bvt16_ref = kv_refs
    else:
        ka32_ref, va32_ref, ik32_ref, bk32_ref, bv32_ref = kv_refs
    h = _rms(x_ref[...], g_ref[...]).astype(BF16)
    cos, sin = cos_ref[...], sin_ref[...]
    lane = lax.broadcasted_iota(I32, cos.shape, 1)
    first_half = (lane & 63) < 32

    def rope(y):
        rot = jnp.where(first_half, pltpu.roll(y, LANES - 32, 1), pltpu.roll(y, 32, 1))
        return y * cos + rot * sin

    def seg(name):
        c0, width = PROJ_COLS[name]
        return _dot_nt(h, w_ref[c0:c0 + width, :])

    def groups(y):
        return [y[:, LANES * j:LANES * (j + 1)] for j in range(y.shape[1] // LANES)]

    for j, y in enumerate(groups(seg("aq"))):
        qa_ref[:, LANES * j:LANES * (j + 1)] = (rope(y) * 0.125).astype(BF16)
    for j, y in enumerate(groups(seg("iq"))):
        qi_ref[:, LANES * j:LANES * (j + 1)] = (rope(y) * 0.125).astype(BF16)
    for j, y in enumerate(groups(seg("bq"))):
        bq_ref[:, LANES * j:LANES * (j + 1)] = (rope(y) * 0.125).astype(BF16)
    ka = rope(seg("ak"))
    va = seg("av")
    y = seg("ik")
    ik = jnp.where(lane < 64, rope(y), y)
    bk = [rope(y) for y in groups(seg("bk"))]
    bv = groups(seg("bv"))
    if not prompt:
        ka32_ref[...] = ka
        va32_ref[...] = va
        ik32_ref[...] = ik
        for j in range(B_HEADS):
            bk32_ref[:, LANES * j:LANES * (j + 1)] = bk[j]
            bv32_ref[:, LANES * j:LANES * (j + 1)] = bv[j]
        return
    tm = ka.shape[0]
    kat32_ref[...] = ka.T
    ka16_ref[...] = ka.astype(BF16)
    vat = va.T
    vat32_ref[...] = vat
    vat16_ref[...] = vat.astype(BF16)
    ikt32_ref[...] = ik.T
    ik16_ref[...] = ik.astype(BF16)
    for j in range(B_HEADS):
        rows = pl.ds(j, tm, stride=B_HEADS)
        bk32_ref[rows, :] = bk[j]
        bv32_ref[rows, :] = bv[j]
        bk16_ref[:, LANES * j:LANES * (j + 1)] = bk[j].astype(BF16)
        bvt16_ref[LANES * j:LANES * (j + 1), :] = bv[j].T.astype(BF16)


def _proj(x, g, w, cos, sin, *, prompt, n_batch=1):
    rows, d = x.shape
    seq = cos.shape[0]
    tm = _row_tile(seq)
    n_t = seq // tm
    row = lambda width: pl.BlockSpec((tm, width), lambda i: (i, 0))
    tab = pl.BlockSpec((tm, LANES), lambda i: (i % n_t, 0))
    rm = lambda width, dt: (row(width), jax.ShapeDtypeStruct((rows, width), dt))
    outs = dict(qa=rm(1024, BF16), qi=rm(512, BF16), bq=rm(1024, BF16))
    if prompt:
        fm = lambda feat, dt: (pl.BlockSpec((None, feat, tm), lambda i: (i // n_t, 0, i % n_t)),
                               jax.ShapeDtypeStruct((n_batch, feat, seq), dt))
        tiled = lambda feat, dt: (pl.BlockSpec((None, feat, tm), lambda i: (i, 0, 0)),
                                  jax.ShapeDtypeStruct((rows // tm, feat, tm), dt))
        head_rows = (pl.BlockSpec((tm * B_HEADS, LANES), lambda i: (i, 0)),
                     jax.ShapeDtypeStruct((rows * B_HEADS, LANES), F32))
        outs.update(kat32=fm(LANES, F32), vat32=fm(LANES, F32), ikt32=fm(LANES, F32), bk32=head_rows,
                    bv32=head_rows, ka16=rm(LANES, BF16), ik16=rm(LANES, BF16), bk16=rm(512, BF16),
                    vat16=tiled(LANES, BF16), bvt16=tiled(512, BF16))
    else:
        outs.update(ka32=rm(LANES, F32), va32=rm(LANES, F32), ik32=rm(LANES, F32), bk32=rm(512, F32),
                    bv32=rm(512, F32))
    out = pl.pallas_call(
        functools.partial(_proj_body, prompt=prompt), grid=(rows // tm,),
        in_specs=[row(d), _resident((1, d)), _resident(w.shape), tab, tab],
        out_specs=[spec for spec, _ in outs.values()],
        out_shape=[shape for _, shape in outs.values()],
        compiler_params=_cparams(("arbitrary",)), name="proj_prompt" if prompt else "proj_sample",
    )(x, g.reshape(1, d), w, cos, sin)
    return dict(zip(outs.keys(), out))


def _memkv_body(x_ref, g_ref, w_ref, k32_ref, v32_ref, k16_ref, v16_ref):
    h = _rms(x_ref[...], g_ref[...]).astype(BF16)
    y = _dot(h, w_ref[...])
    half = y.shape[1] // 2
    k, v = y[:, :half], y[:, half:]
    k32_ref[...] = k
    v32_ref[...] = v
    k16_ref[...] = k.astype(BF16)
    v16_ref[...] = v.astype(BF16)


def _memkv(x, g, w):
    rows, d = x.shape
    half = w.shape[1] // 2
    tm = _row_tile(rows)
    row = lambda width: pl.BlockSpec((tm, width), lambda i: (i, 0))
    return pl.pallas_call(
        _memkv_body, grid=(rows // tm,),
        in_specs=[row(d), _resident((1, d)), _resident(w.shape)],
        out_specs=[row(half)] * 4,
        out_shape=[jax.ShapeDtypeStruct((rows, half), dt) for dt in (F32, F32, BF16, BF16)],
        compiler_params=_cparams(("arbitrary",)), name="memkv",
    )(x, g.reshape(1, d), w)


def _float_key(s):
    bits = pltpu.bitcast(s + 0.0, I32)
    return bits ^ ((bits >> 31) & 0x7FFFFFFF)


def _kth_largest_key(count_ge, rows_shape, k):
    base = jnp.where(count_ge(jnp.zeros(rows_shape, I32)) >= k, 0, INT_MIN).astype(I32)

    def bit_step(i, base):
        cand = base | (jnp.int32(1) << (30 - i))
        return jnp.where(count_ge(cand) >= k, cand, base)

    return lax.fori_loop(0, 31, bit_step, base)


I16 = jnp.int16


def _pack16(x):
    half = x.shape[0] // 2
    return (x[:half] & 0xFFFF) | (x[half:] << 16)


def _kth_largest_16(count_ge, rows_shape, k):
    word = lambda c: (c & 0xFFFF) | (c << 16)
    base = jnp.where(count_ge(jnp.zeros(rows_shape, I32)) >= k, 0, -(2 ** 15)).astype(I32)

    def bit_step(i, base):
        cand = base | (jnp.int32(1) << (14 - i))
        return jnp.where(count_ge(word(cand)) >= k, cand, base)

    return lax.fori_loop(0, 15, bit_step, base)


def _softmax_step(s, m_prev, l_prev, axis=-1):
    m_new = jnp.maximum(m_prev, jnp.max(s, axis=axis, keepdims=True))
    alpha = jnp.exp(m_prev - m_new)
    p = jnp.exp(s - m_new)
    l_new = alpha * l_prev + jnp.sum(p, axis=axis, keepdims=True)
    return p, m_new, l_new, alpha


def _dsa_body(qa_ref, qi_ref, ikt_ref, ka_ref, ik_ref, vat_ref, tril_ref, o_ref, key_scr, half_scr, acc_scr,
              s_scr, *, tq, tk, top_k):
    q0 = pl.program_id(1) * tq
    n_kb = (q0 + tq + tk - 1) // tk
    rep = A_HEADS // A_KV_HEADS
    q_pos = q0 + lax.broadcasted_iota(I32, (1, tq), 1)
    k_iota = lax.broadcasted_iota(I32, (tk, 1), 0)

    qi = jnp.concatenate([qi_ref[:, LANES * h:LANES * (h + 1)] for h in range(IDX_HEADS)], axis=0)
    w_heads = [ikt_ref[IDX_DIM + h:IDX_DIM + h + 1, :] for h in range(IDX_HEADS)]

    def score_block(kb, carry):
        k0 = pl.multiple_of(kb * tk, tk)
        logits = _dot_nt(ik_ref[pl.ds(k0, tk), :], qi)
        s = jnp.zeros((tk, tq), F32)
        for h in range(IDX_HEADS):
            s = s + jnp.maximum(logits[:, h * tq:(h + 1) * tq], 0.0) * w_heads[h]
        s = jnp.where(k0 + k_iota <= q_pos, s, -jnp.inf)
        key = _float_key(s)
        key_scr[kb] = key
        half_scr[kb] = _pack16(key >> 16)
        return carry

    lax.fori_loop(0, n_kb, score_block, 0)

    def count_ge(cand):
        def blk(kb, acc):
            c = jnp.where(key_scr[kb] >= cand, 1, 0)
            return acc + jnp.sum(c.reshape(tk // 8, 8, tq), axis=0)
        acc = lax.fori_loop(0, n_kb, blk, jnp.zeros((8, tq), I32))
        return jnp.sum(acc, axis=0, keepdims=True)

    def count_ge_16(cand_word):
        cand = pltpu.bitcast(jnp.broadcast_to(cand_word, (8, tq)), I16)
        def blk(kb, acc):
            v = pltpu.bitcast(half_scr[kb], I16)
            ones = [jnp.where(v[16 * i:16 * (i + 1)] >= cand, jnp.int16(1), jnp.int16(0)) for i in range(tk // 16)]
            while len(ones) > 1:
                ones = [a + b for a, b in zip(ones[0::2], ones[1::2])]
            return acc + ones[0]
        acc = pltpu.bitcast(lax.fori_loop(0, n_kb, blk, jnp.zeros((16, tq), I16)), I32)
        return jnp.sum((acc & 0xFFFF) + (acc >> 16), axis=0, keepdims=True)

    thr_hi = _kth_largest_16(count_ge_16, (1, tq), top_k)

    def low_halves(kb, carry):
        key = key_scr[kb]
        hi = key >> 16
        low = (key & 0xFFFF) - 2 ** 15
        half_scr[kb] = _pack16(jnp.where(hi == thr_hi, low, jnp.where(hi > thr_hi, 2 ** 15 - 1, -(2 ** 15))))
        return carry

    lax.fori_loop(0, n_kb, low_halves, 0)
    thr_lo = _kth_largest_16(count_ge_16, (1, tq), top_k)
    thr = (thr_hi << 16) | ((thr_lo + 2 ** 15) & 0xFFFF)
    need = (top_k - count_ge(thr + 1)).astype(F32)
    thr_tie = jnp.where(thr == KEY_NEG_INF, jnp.int32(2 ** 31 - 1), thr)

    q_groups = [jnp.concatenate([qa_ref[:, LANES * (g * rep + r):LANES * (g * rep + r + 1)] for r in range(rep)], axis=0)
                for g in range(A_KV_HEADS)]
    acc_scr[...] = jnp.zeros(acc_scr.shape, F32)

    def logits(kb, g):
        k0 = pl.multiple_of(kb * tk, tk)
        s_scr[g] = _dot_nt(ka_ref[pl.ds(k0, tk), :], q_groups[g])

    def mask_bias(kb, tie_seen):
        key = key_scr[kb]
        tie = key == thr_tie
        rank = tie_seen + _dot(tril_ref[...], jnp.where(tie, 1.0, 0.0).astype(BF16))
        bias = jnp.where(key > thr, 0.0, jnp.where(tie, jnp.where(rank <= need, 0.0, NEG_BIG), NEG_BIG))
        return bias, rank[tk - 1:tk, :]

    def attend(kb, g, bias, m, l):
        s = s_scr[g]
        s = jnp.concatenate([s[:, r * tq:(r + 1) * tq] + bias for r in range(rep)], axis=1)
        p, m, l, alpha = _softmax_step(s, m, l, axis=0)
        v_g = vat_ref[kb, A_HEAD_DIM * g:A_HEAD_DIM * (g + 1), :]
        acc_scr[g] = alpha * acc_scr[g] + _dot(v_g, p.astype(BF16))
        return m, l

    def block(kb, carry, more_blocks):
        m0, l0, m1, l1, tie_seen = carry
        logits(kb, 1)
        bias, tie_seen = mask_bias(kb, tie_seen)
        m0, l0 = attend(kb, 0, bias, m0, l0)
        if more_blocks:
            logits(kb + 1, 0)
        m1, l1 = attend(kb, 1, bias, m1, l1)
        return m0, l0, m1, l1, tie_seen

    stats = (1, rep * tq)
    neg, zero = jnp.full(stats, NEG_BIG, F32), jnp.zeros(stats, F32)
    logits(0, 0)
    carry = lax.fori_loop(0, n_kb - 1, lambda kb, c: block(kb, c, True), (neg, zero, neg, zero, jnp.zeros((1, tq), F32)))
    _, l0, _, l1, _ = block(n_kb - 1, carry, False)
    l_groups = (l0, l1)

    for g in range(A_KV_HEADS):
        o = acc_scr[g] / l_groups[g]
        for j in range(rep // 2):
            pair = jnp.concatenate([o[:, (2 * j) * tq:(2 * j + 1) * tq], o[:, (2 * j + 1) * tq:(2 * j + 2) * tq]], axis=0)
            c = g * (rep // 2) + j
            o_ref[:, LANES * c:LANES * (c + 1)] = pair.T.astype(BF16)


def _dsa_prompt(p, n_batch, seq, top_k):
    tq = min(128, seq)
    tk = p["vat16"].shape[2]
    rep = A_HEADS // A_KV_HEADS
    n_q = seq // tq
    tril = jnp.tril(jnp.ones((tk, tk), BF16))
    qrow = lambda width: pl.BlockSpec((tq, width), lambda b, i: (b * n_q + i, 0))
    seqblk = lambda width: pl.BlockSpec((seq, width), lambda b, i: (b, 0))
    return pl.pallas_call(
        functools.partial(_dsa_body, tq=tq, tk=tk, top_k=top_k),
        grid=(n_batch, n_q),
        in_specs=[qrow(1024), qrow(512), pl.BlockSpec((None, LANES, tq), lambda b, i: (b, 0, i)),
                  seqblk(LANES), seqblk(LANES), pl.BlockSpec((seq // tk, LANES, tk), lambda b, i: (b, 0, 0)),
                  _resident((tk, tk))],
        out_specs=qrow(512),
        out_shape=jax.ShapeDtypeStruct((n_batch * seq, 512), BF16),
        scratch_shapes=[pltpu.VMEM((seq // tk, tk, tq), I32),
                        pltpu.VMEM((seq // tk, tk // 2, tq), I32),
                        pltpu.VMEM((A_KV_HEADS, A_HEAD_DIM, rep * tq), F32),
                        pltpu.VMEM((A_KV_HEADS, tk, rep * tq), F32)],
        compiler_params=_cparams(("arbitrary", "arbitrary")), name="dsa_prompt",
    )(p["qa"], p["qi"], p["ikt32"], p["ka16"], p["ik16"], p["vat16"], tril)


def _diff_lambda(lq1, lk1, lq2, lk2, lambda_init):
    s1 = jnp.sum(lq1 * lk1, axis=-1, keepdims=True)
    s2 = jnp.sum(lq2 * lk2, axis=-1, keepdims=True)
    return jnp.exp(s1) - jnp.exp(s2) + lambda_init


def _subln(o, gain, lambda_init):
    return _rms(o, gain) * (1.0 - lambda_init)


def _diff_body(bq_ref, bk_ref, bvt_ref, lq1_ref, lk1_ref, lq2_ref, lk2_ref, gain_ref, o_ref, acc_scr, s_scr,
               *, tq, tk, lambda_init):
    q0 = pl.program_id(1) * tq
    n_full = q0 // tk
    assert tq <= tk
    qp = q0 + lax.broadcasted_iota(I32, (1, tq), 1)
    q_pos = jnp.concatenate([qp, qp], axis=1)
    k_iota = lax.broadcasted_iota(I32, (tk, 1), 0)
    acc_scr[...] = jnp.zeros(acc_scr.shape, F32)
    q_heads = [jnp.concatenate([bq_ref[:, LANES * (2 * h + c):LANES * (2 * h + c + 1)] for c in range(2)], axis=0)
               for h in range(B_HEADS)]

    def logits(kb, h):
        k0 = pl.multiple_of(kb * tk, tk)
        s_scr[h % 2] = _dot_nt(bk_ref[pl.ds(k0, tk), LANES * h:LANES * (h + 1)], q_heads[h])

    def attend(kb, h, m, l, masked):
        s = s_scr[h % 2]
        if masked:
            s = jnp.where(kb * tk + k_iota <= q_pos, s, NEG_BIG)
        p, m, l, alpha = _softmax_step(s, m, l, axis=0)
        vt = bvt_ref[kb, LANES * h:LANES * (h + 1), :]
        acc_scr[h] = alpha * acc_scr[h] + _dot(vt, p.astype(BF16))
        return m, l

    def block(kb, stats, masked):
        out = []
        for h in range(B_HEADS):
            if h + 1 < B_HEADS:
                logits(kb, h + 1)
            elif not masked:
                logits(kb + 1, 0)
            out += attend(kb, h, stats[2 * h], stats[2 * h + 1], masked)
        return tuple(out)

    shape = (1, 2 * tq)
    init = tuple(jnp.full(shape, NEG_BIG, F32) if i % 2 == 0 else jnp.zeros(shape, F32) for i in range(2 * B_HEADS))
    logits(0, 0)
    stats = lax.fori_loop(0, n_full, lambda kb, st: block(kb, st, False), init)
    stats = block(n_full, stats, True)

    lam = _diff_lambda(lq1_ref[...], lk1_ref[...], lq2_ref[...], lk2_ref[...], lambda_init)
    for h in range(B_HEADS):
        o = acc_scr[h] / stats[2 * h + 1]
        o = (o[:, :tq] - lam * o[:, tq:]).T
        o_ref[:, LANES * h:LANES * (h + 1)] = _subln(o, gain_ref[...], lambda_init).astype(BF16)


def _diff_prompt(p, lam_params, gain, n_batch, seq, lambda_init):
    tk = p["bvt16"].shape[2]
    tq = min(256, seq)
    assert tk % tq == 0 or tq % tk == 0
    n_q = seq // tq
    qrow = lambda width: pl.BlockSpec((tq, width), lambda b, i: (b * n_q + i, 0))
    small = [_resident((1, a.shape[-1])) for a in lam_params] + [_resident((1, B_V_DIM))]
    return pl.pallas_call(
        functools.partial(_diff_body, tq=tq, tk=tk, lambda_init=lambda_init),
        grid=(n_batch, n_q),
        in_specs=[qrow(1024), pl.BlockSpec((seq, 512), lambda b, i: (b, 0)),
                  pl.BlockSpec((seq // tk, 512, tk), lambda b, i: (b, 0, 0))] + small,
        out_specs=qrow(512),
        out_shape=jax.ShapeDtypeStruct((n_batch * seq, 512), BF16),
        scratch_shapes=[pltpu.VMEM((B_HEADS, B_V_DIM, 2 * tq), F32), pltpu.VMEM((2, tk, 2 * tq), F32)],
        compiler_params=_cparams(("arbitrary", "arbitrary")), name="diff_prompt",
    )(p["bq"], p["bk16"], p["bvt16"], *[a.reshape(1, -1) for a in lam_params], gain.reshape(1, -1))


def _post_body(*refs, mem_in_kernel):
    if mem_in_kernel:
        (x_ref, g_ref, oa_ref, ob_ref, mk_ref, mv_ref, wmq_ref, wg_ref, wpa_ref, wpb_ref, wpm_ref, wo_ref,
         o_ref) = refs
    else:
        x_ref, g_ref, oa_ref, ob_ref, om_ref, wg_ref, wpa_ref, wpb_ref, wpm_ref, wo_ref, o_ref = refs
    x = x_ref[...]
    d = x.shape[1]
    h = _rms(x, g_ref[...]).astype(BF16)
    if mem_in_kernel:
        mq = (_dot_nt(h, wmq_ref[...]) * (M_HEAD_DIM ** -0.5)).astype(BF16)
        heads = []
        for hd in range(M_HEADS):
            sl = slice(LANES * hd, LANES * (hd + 1))
            s = _dot_nt(mq[:, sl], mk_ref[:, sl])
            e = jnp.exp(s - jnp.max(s, axis=-1, keepdims=True))
            o = _dot(e.astype(BF16), mv_ref[:, sl]) / jnp.sum(e, axis=-1, keepdims=True)
            heads.append(o.astype(BF16))
        om = jnp.concatenate(heads, axis=1)
    else:
        om = om_ref[...]

    def gate(i):
        return jax.nn.sigmoid(_dot_nt(h, wg_ref[d * i:d * (i + 1), :]))

    merged = gate(0) * _dot(oa_ref[...], wpa_ref[...])
    merged = merged + gate(1) * _dot(ob_ref[...], wpb_ref[...])
    merged = merged + gate(2) * _dot(om, wpm_ref[...])
    o_ref[...] = x + _dot(merged.astype(BF16), wo_ref[...])


def _post(x, g, oa, ob, w, *, mem=None, om=None, n_batch=1):
    rows, d = x.shape
    seq = rows // n_batch
    tm = _row_tile(seq)
    n_t = seq // tm
    row = lambda width: pl.BlockSpec((tm, width), lambda b, i: (b * n_t + i, 0))
    weights = [w["gates"], w["proj_a"], w["proj_b"], w["proj_m"], w["out"]]
    if mem is not None:
        mk, mv, w_mq = mem
        memblk = pl.BlockSpec((MEM_TOKENS, mk.shape[1]), lambda b, i: (b, 0))
        args = [x, g.reshape(1, d), oa, ob, mk, mv, w_mq] + weights
        in_specs = ([row(d), _resident((1, d)), row(512), row(512), memblk, memblk, _resident(w_mq.shape)]
                    + [_resident(a.shape) for a in weights])
    else:
        args = [x, g.reshape(1, d), oa, ob, om] + weights
        in_specs = [row(d), _resident((1, d)), row(512), row(512), row(512)] + [_resident(a.shape) for a in weights]
    return pl.pallas_call(
        functools.partial(_post_body, mem_in_kernel=mem is not None),
        grid=(n_batch, n_t), in_specs=in_specs, out_specs=row(d),
        out_shape=jax.ShapeDtypeStruct((rows, d), F32),
        compiler_params=_cparams(("arbitrary", "arbitrary")), name="post",
    )(*args)


def _mem_sample_body(x_ref, g_ref, wmq_ref, mk_ref, mv_ref, o_ref, mq_scr):
    b = pl.program_id(0)

    @pl.when(b == 0)
    def _():
        h = _rms(x_ref[...], g_ref[...]).astype(BF16)
        mq_scr[...] = _dot_nt(h, wmq_ref[...]) * (M_HEAD_DIM ** -0.5)

    q = mq_scr[pl.ds(b, 1), :]
    for hd in range(M_HEADS):
        sl = slice(LANES * hd, LANES * (hd + 1))
        rows = pl.ds(hd, MEM_TOKENS, stride=M_HEADS)
        s = jnp.sum(mk_ref[rows, :] * q[:, sl], axis=-1, keepdims=True)
        e = jnp.exp(s - jnp.max(s, axis=0, keepdims=True))
        o = jnp.sum(e * mv_ref[rows, :], axis=0, keepdims=True) / jnp.sum(e, axis=0, keepdims=True)
        o_ref[:, sl] = o.astype(BF16)


def _mem_sample(x, g, w_mq, cache_k, cache_v, layer):
    n, d = x.shape
    width = w_mq.shape[0]
    memblk = pl.BlockSpec((None, None) + cache_k.shape[2:], lambda b: (b, layer, 0, 0))
    return pl.pallas_call(
        _mem_sample_body, grid=(n,),
        in_specs=[_resident((n, d)), _resident((1, d)), _resident(w_mq.shape), memblk, memblk],
        out_specs=pl.BlockSpec((None, 1, width), lambda b: (b, 0, 0)),
        out_shape=jax.ShapeDtypeStruct((n, 1, width), BF16),
        scratch_shapes=[pltpu.VMEM((n, width), F32)],
        compiler_params=_cparams(("arbitrary",)), name="mem_sample",
    )(x, g.reshape(1, d), w_mq, cache_k, cache_v)


def _page_copies(pt_ref, b, chunk, pages_per_chunk, layer, streams):
    slot = chunk % 2
    out = []
    for cache, dst, sem in streams:
        for j in range(pages_per_chunk):
            page = pt_ref[b, chunk * pages_per_chunk + j]
            out.append(pltpu.make_async_copy(cache.at[page, layer], dst(slot, j), sem.at[slot]))
    return out


def _lane_pages(buf, page):
    return lambda slot, j: buf.at[slot, :, pl.ds(page * j, page)]


def _row_pages(buf, rows):
    return lambda slot, j: buf.at[slot, pl.ds(rows * j, rows), :]


def _chunks(n_pages):
    ppc = min(16, n_pages)
    assert n_pages % ppc == 0
    return ppc, n_pages // ppc


def _for_each_chunk(pt_ref, n_pages, layer, streams, body):
    b = pl.program_id(0)
    n_samples = pl.num_programs(0)
    ppc, n_ch = _chunks(n_pages)
    across_samples = n_ch % 2 == 0

    def start(sample, c):
        for cp in _page_copies(pt_ref, sample, c, ppc, layer, streams):
            cp.start()

    if across_samples:
        pl.when(b == 0)(lambda: start(b, 0))
    else:
        start(b, 0)
    for c in range(n_ch):
        if c + 1 < n_ch:
            start(b, c + 1)
        elif across_samples:
            pl.when(b + 1 < n_samples)(lambda: start(b + 1, 0))
        for cp in _page_copies(pt_ref, b, c, ppc, layer, streams):
            cp.wait()
        body(c)


def _idx_sample_body(pt_ref, qi_ref, w_ref, ikn_ref, cache_ref, s_ref, buf, sem, *, layer, n_pages, page):
    ppc, n_ch = _chunks(n_pages)
    ck = ppc * page
    q = qi_ref[:, 0:IDX_DIM]
    w = w_ref[...]

    def scores(logits):
        return jnp.sum(jnp.maximum(logits, 0.0) * w, axis=0, keepdims=True)

    def chunk(c):
        s_ref[:, ck * c:ck * (c + 1)] = scores(_dot(q, buf[c % 2].astype(BF16)))

    _for_each_chunk(pt_ref, n_pages, layer, [(cache_ref, _lane_pages(buf, page), sem)], chunk)
    lane = lax.broadcasted_iota(I32, (1, LANES), 1)
    new_key = ikn_ref[:, 0:IDX_DIM].astype(BF16).astype(F32)
    s_new = scores(jnp.sum(q.astype(F32) * new_key, axis=1, keepdims=True))
    s_ref[:, ck * n_ch:] = jnp.where(lane == 0, s_new, -jnp.inf)


def _idx_sample(page_table, qi8, w8, ik_new, cache_t, layer):
    n, n_pages = page_table.shape
    feat, page = cache_t.shape[2], cache_t.shape[3]
    ppc, n_ch = _chunks(n_pages)
    nk = n_pages * page + LANES
    grid_spec = pltpu.PrefetchScalarGridSpec(
        num_scalar_prefetch=1, grid=(n,),
        in_specs=[pl.BlockSpec((None, 8, LANES), lambda b, pt: (b, 0, 0)),
                  pl.BlockSpec((None, 8, 1), lambda b, pt: (b, 0, 0)),
                  pl.BlockSpec((None, 1, LANES), lambda b, pt: (b, 0, 0)),
                  pl.BlockSpec(memory_space=pl.ANY)],
        out_specs=pl.BlockSpec((None, 1, nk), lambda b, pt: (b, 0, 0)),
        scratch_shapes=[pltpu.VMEM((2, feat, ppc * page), F32), pltpu.SemaphoreType.DMA((2,))])
    return pl.pallas_call(
        functools.partial(_idx_sample_body, layer=layer, n_pages=n_pages, page=page),
        grid_spec=grid_spec, out_shape=jax.ShapeDtypeStruct((n, 1, nk), F32),
        compiler_params=_cparams(("arbitrary",)), name="idx_sample",
    )(page_table, qi8, w8, ik_new, cache_t)


def _select_body(s_ref, bias_ref, *, top_k, n_valid):
    key = _float_key(s_ref[...])
    rows, nk = key.shape
    shape = (rows, 1)

    def count(ones):
        return jnp.sum(ones, axis=1, keepdims=True)

    thr = _kth_largest_key(lambda cand: count(jnp.where(key >= cand, 1, 0)), shape, top_k)
    need = top_k - count(jnp.where(key > thr, 1, 0))
    thr_tie = jnp.where(thr == KEY_NEG_INF, jnp.int32(2 ** 31 - 1), thr)
    tie = jnp.where(key == thr_tie, 1, 0)
    idx = lax.broadcasted_iota(I32, key.shape, 1)
    n_bits = max(1, (nk - 1).bit_length())

    def bit_step(i, last):
        cand = last - (jnp.int32(1) << (n_bits - 1 - i))
        return jnp.where(count(jnp.where(idx <= cand, tie, 0)) >= need, cand, last)

    last = lax.fori_loop(0, n_bits, bit_step, jnp.full(shape, 2 ** n_bits - 1, I32))
    tie_bias = jnp.where(idx <= last, jnp.where(tie > 0, 0.0, NEG_BIG), NEG_BIG)
    bias = jnp.where(key > thr, 0.0, tie_bias)
    bias_ref[...] = jnp.where(idx < n_valid, bias, NEG_BIG)


def _select(scores, top_k, n_valid):
    return pl.pallas_call(
        functools.partial(_select_body, top_k=top_k, n_valid=n_valid),
        out_shape=jax.ShapeDtypeStruct(scores.shape, F32),
        compiler_params=pltpu.CompilerParams(vmem_limit_bytes=V7X_VMEM_LIMIT_BYTES), name="select_sample",
    )(scores)


def _bf16_round(x):
    return x.astype(BF16).astype(F32)


def _new_row_step(s_new, v_new, m, l, acc):
    m_new = jnp.maximum(m, s_new)
    alpha = jnp.exp(m - m_new)
    p_new = jnp.exp(s_new - m_new)
    return alpha * l + p_new, alpha * acc + p_new * v_new


def _dsa_sample_body(pt_ref, q_ref, kn_ref, vn_ref, bias_ref, kc_ref, vc_ref, o_ref, kbuf, vbuf, ksem, vsem,
                     *, layer, n_pages, page):
    ppc, n_ch = _chunks(n_pages)
    ck = ppc * page
    q = q_ref[...]
    st = dict(m=jnp.full((8, 1), NEG_BIG, F32), l=jnp.zeros((8, 1), F32), acc=jnp.zeros((8, LANES), F32))

    def chunk(c):
        s = _dot(q, kbuf[c % 2].astype(BF16)) + bias_ref[:, ck * c:ck * (c + 1)]
        p, st["m"], st["l"], alpha = _softmax_step(s, st["m"], st["l"])
        st["acc"] = alpha * st["acc"] + _dot_nt(p.astype(BF16), vbuf[c % 2].astype(BF16))

    _for_each_chunk(pt_ref, n_pages, layer,
                    [(kc_ref, _lane_pages(kbuf, page), ksem), (vc_ref, _lane_pages(vbuf, page), vsem)], chunk)
    s_new = (jnp.sum(q.astype(F32) * _bf16_round(kn_ref[...]), axis=1, keepdims=True)
             + bias_ref[:, ck * n_ch:ck * n_ch + 1])
    l, acc = _new_row_step(s_new, _bf16_round(vn_ref[...]), st["m"], st["l"], st["acc"])
    o_ref[...] = acc / l


def _diff_sample_body(pt_ref, q_ref, kn_ref, vn_ref, kc_ref, vc_ref, lq1_ref, lk1_ref, lq2_ref, lk2_ref,
                      gain_ref, o_ref, kbuf, vbuf, ksem, vsem, *, layer, n_pages, page, lambda_init):
    ppc, n_ch = _chunks(n_pages)
    ck = ppc * page
    page_rows = page * B_HEADS
    q = q_ref[...]
    row_head = lax.broadcasted_iota(I32, (8, 1), 0) >> 1
    st = dict(m=jnp.full((8, 1), NEG_BIG, F32), l=jnp.zeros((8, 1), F32), acc=jnp.zeros((8, B_HEADS * LANES), F32))

    def chunk(c):
        head_rows = lambda buf, h: buf[c % 2, pl.ds(h, ck, stride=B_HEADS), :].astype(BF16)
        s = jnp.zeros((8, ck), F32)
        for h in range(B_HEADS):
            s = jnp.where(row_head == h, _dot_nt(q, head_rows(kbuf, h)), s)
        p, st["m"], st["l"], alpha = _softmax_step(s, st["m"], st["l"])
        pb = p.astype(BF16)
        pv = jnp.concatenate([_dot(pb, head_rows(vbuf, h)) for h in range(B_HEADS)], axis=1)
        st["acc"] = alpha * st["acc"] + pv

    _for_each_chunk(pt_ref, n_pages, layer,
                    [(kc_ref, _row_pages(kbuf, page_rows), ksem), (vc_ref, _row_pages(vbuf, page_rows), vsem)], chunk)
    qf = q.astype(F32)
    kn = _bf16_round(kn_ref[...])
    s_new = jnp.zeros((8, 1), F32)
    for h in range(B_HEADS):
        s_h = jnp.sum(qf * kn[:, LANES * h:LANES * (h + 1)], axis=1, keepdims=True)
        s_new = jnp.where(row_head == h, s_h, s_new)
    l, acc = _new_row_step(s_new, _bf16_round(vn_ref[...]), st["m"], st["l"], st["acc"])
    o = acc / l
    lam = _diff_lambda(lq1_ref[...], lk1_ref[...], lq2_ref[...], lk2_ref[...], lambda_init)
    for h in range(B_HEADS):
        sl = slice(LANES * h, LANES * (h + 1))
        oh = o[2 * h:2 * h + 1, sl] - lam * o[2 * h + 1:2 * h + 2, sl]
        o_ref[:, sl] = _subln(oh, gain_ref[...], lambda_init).astype(BF16)


def _per_sample(a):
    return pl.BlockSpec((None,) + a.shape[1:], lambda b, pt: (b,) + (0,) * (a.ndim - 1))


def _dsa_sample(page_table, q8, k_new, v_new, bias, cache_kt, cache_vt, layer):
    n, n_pages = page_table.shape
    feat, page = cache_kt.shape[2], cache_kt.shape[3]
    ppc, _ = _chunks(n_pages)
    any_spec = pl.BlockSpec(memory_space=pl.ANY)
    grid_spec = pltpu.PrefetchScalarGridSpec(
        num_scalar_prefetch=1, grid=(n,),
        in_specs=[_per_sample(q8), _per_sample(k_new), _per_sample(v_new), _per_sample(bias), any_spec, any_spec],
        out_specs=pl.BlockSpec((None, 8, feat), lambda b, pt: (b, 0, 0)),
        scratch_shapes=[pltpu.VMEM((2, feat, ppc * page), F32), pltpu.VMEM((2, feat, ppc * page), F32),
                        pltpu.SemaphoreType.DMA((2,)), pltpu.SemaphoreType.DMA((2,))])
    return pl.pallas_call(
        functools.partial(_dsa_sample_body, layer=layer, n_pages=n_pages, page=page),
        grid_spec=grid_spec, out_shape=jax.ShapeDtypeStruct((n, 8, feat), F32),
        compiler_params=_cparams(("arbitrary",)), name="dsa_sample",
    )(page_table, q8, k_new, v_new, bias, cache_kt, cache_vt)


def _diff_sample(page_table, q8, k_new, v_new, cache_k, cache_v, layer, diff_params, lambda_init):
    n, n_pages = page_table.shape
    page_rows = cache_k.shape[2]
    ppc, _ = _chunks(n_pages)
    width = B_HEADS * B_V_DIM
    any_spec = pl.BlockSpec(memory_space=pl.ANY)
    extra = [a.reshape(1, -1) for a in diff_params]
    grid_spec = pltpu.PrefetchScalarGridSpec(
        num_scalar_prefetch=1, grid=(n,),
        in_specs=([_per_sample(q8), _per_sample(k_new), _per_sample(v_new), any_spec, any_spec]
                  + [pl.BlockSpec(a.shape, lambda b, pt: (0, 0)) for a in extra]),
        out_specs=pl.BlockSpec((None, 1, width), lambda b, pt: (b, 0, 0)),
        scratch_shapes=[pltpu.VMEM((2, ppc * page_rows, LANES), F32), pltpu.VMEM((2, ppc * page_rows, LANES), F32),
                        pltpu.SemaphoreType.DMA((2,)), pltpu.SemaphoreType.DMA((2,))])
    return pl.pallas_call(
        functools.partial(_diff_sample_body, layer=layer, n_pages=n_pages, page=page_rows // B_HEADS,
                          lambda_init=lambda_init),
        grid_spec=grid_spec, out_shape=jax.ShapeDtypeStruct((n, 1, width), BF16),
        compiler_params=_cparams(("arbitrary",)), name="diff_sample",
    )(page_table, q8, k_new, v_new, cache_k, cache_v, *extra)


def _layer_weights(l, w_in_t, w_proj_a, w_proj_b, w_proj_m, w_out):
    o_mq = sum((512, 128, 128, 256, 64, 4, 512, 512, 512))
    wt = w_in_t[:, l, :]
    return dict(proj=_proj_weight(wt),
                mq=wt[o_mq:o_mq + 512].astype(BF16),
                gates=wt[o_mq + 512:].astype(BF16),
                proj_a=w_proj_a[l].astype(BF16), proj_b=w_proj_b[l].astype(BF16),
                proj_m=w_proj_m[l].astype(BF16), out=w_out[l].astype(BF16))


def kernel(x_prompt, x_sample, cache_a_k, cache_a_v, cache_idx_k, cache_b_k, cache_b_v, cache_mem_k, cache_mem_v, page_table, mem_prompt, norm_ffn1, ffn1_w_in, ffn1_w_out, norm_mix, w_in, lambda_q1, lambda_k1, lambda_q2, lambda_k2, subln_gain, norm_mem, w_mem_kv, w_proj_a, w_proj_b, w_proj_m, w_out, norm_ffn2, ffn2_w_in, ffn2_w_out, final_norm):
    n_b, seq, d = x_prompt.shape
    n_s, dec_seq, _ = x_sample.shape
    assert dec_seq == 1
    depth = norm_ffn1.shape[0]
    n_pool, _, page = cache_a_k.shape[:3]
    n_pages = page_table.shape[1]
    past_len = n_pages * page
    top_k_prompt = min(IDX_TOPK_MAX, seq // 4)
    top_k_sample = min(IDX_TOPK_MAX, (past_len + dec_seq) // 4)

    cos_p, sin_p = _rope_tables(jnp.arange(seq, dtype=I32))
    cos_s, sin_s = _rope_tables(jnp.full((n_s,), past_len, I32))
    c_ik = jnp.transpose(cache_idx_k, (0, 1, 3, 2))
    c_ak = jnp.transpose(cache_a_k, (0, 1, 3, 4, 2)).reshape(n_pool, depth, -1, page)
    c_av = jnp.transpose(cache_a_v, (0, 1, 3, 4, 2)).reshape(n_pool, depth, -1, page)
    c_bk = cache_b_k.reshape(n_pool, depth, page * B_HEADS, -1)
    c_bv = cache_b_v.reshape(n_pool, depth, page * B_HEADS, -1)
    c_mk = cache_mem_k.reshape(n_s, depth, MEM_TOKENS * M_HEADS, -1)
    c_mv = cache_mem_v.reshape(n_s, depth, MEM_TOKENS * M_HEADS, -1)
    mem_rows = mem_prompt.reshape(n_b * MEM_TOKENS, d)
    w_in_t = jnp.transpose(w_in, (2, 0, 1))

    xp = x_prompt.reshape(n_b * seq, d)
    xs = x_sample.reshape(n_s, d)
    rows_p, rows_s, mem_rows_out = [], [], []
    yp = ys = None
    for l in range(depth):
        last = l == depth - 1
        lambda_init = 0.8 - 0.6 * math.exp(-0.3 * l)
        w = _layer_weights(l, w_in_t, w_proj_a, w_proj_b, w_proj_m, w_out)
        f1_in, f1_out = ffn1_w_in[l].astype(BF16), ffn1_w_out[l].astype(BF16)
        f2_in, f2_out = ffn2_w_in[l].astype(BF16), ffn2_w_out[l].astype(BF16)
        lam_params = (lambda_q1[l], lambda_k1[l], lambda_q2[l], lambda_k2[l])

        mk32, mv32, mk16, mv16 = _memkv(mem_rows, norm_mem[l], w_mem_kv[l].astype(BF16))
        mem_rows_out.append((mk32, mv32))
        xp = _ffn(xp, norm_ffn1[l], f1_in, f1_out)
        p = _proj(xp, norm_mix[l], w["proj"], cos_p, sin_p, prompt=True, n_batch=n_b)
        rows_p.append(p)
        oa = _dsa_prompt(p, n_b, seq, top_k_prompt)
        ob = _diff_prompt(p, lam_params, subln_gain[l], n_b, seq, lambda_init)
        xp = _post(xp, norm_mix[l], oa, ob, w, mem=(mk16, mv16, w["mq"]), n_batch=n_b)
        if last:
            xp, yp = _ffn(xp, norm_ffn2[l], f2_in, f2_out, final_norm)
        else:
            xp = _ffn(xp, norm_ffn2[l], f2_in, f2_out)

        xs = _ffn(xs, norm_ffn1[l], f1_in, f1_out)
        s = _proj(xs, norm_mix[l], w["proj"], cos_s, sin_s, prompt=False)
        rows_s.append(s)
        qi8 = jnp.pad(s["qi"].reshape(n_s, IDX_HEADS, LANES), ((0, 0), (0, 8 - IDX_HEADS), (0, 0)))
        w8 = jnp.pad(s["ik32"][:, IDX_DIM:IDX_DIM + IDX_HEADS], ((0, 0), (0, 8 - IDX_HEADS)))[:, :, None]
        scores = _idx_sample(page_table, qi8, w8, s["ik32"].reshape(n_s, 1, LANES), c_ik, l)
        bias = _select(scores.reshape(n_s, -1), top_k_sample, past_len + 1).reshape(n_s, 1, -1)
        oa8 = _dsa_sample(page_table, s["qa"].reshape(n_s, A_HEADS, LANES), s["ka32"].reshape(n_s, 1, -1),
                          s["va32"].reshape(n_s, 1, -1), bias, c_ak, c_av, l)
        rep = A_HEADS // A_KV_HEADS
        oa_s = jnp.concatenate([oa8[:, h, 64 * (h // rep):64 * (h // rep) + 64] for h in range(A_HEADS)],
                               axis=1).astype(BF16)
        ob_s = _diff_sample(page_table, s["bq"].reshape(n_s, 2 * B_HEADS, LANES), s["bk32"].reshape(n_s, 1, -1),
                            s["bv32"].reshape(n_s, 1, -1), c_bk, c_bv, l, lam_params + (subln_gain[l],),
                            lambda_init).reshape(n_s, -1)
        om_s = _mem_sample(xs, norm_mix[l], w["mq"], c_mk, c_mv, l).reshape(n_s, -1)
        xs = _post(xs, norm_mix[l], oa_s, ob_s, w, om=om_s)
        if last:
            xs, ys = _ffn(xs, norm_ffn2[l], f2_in, f2_out, final_norm)
        else:
            xs = _ffn(xs, norm_ffn2[l], f2_in, f2_out)

    def stack(rows, name, n, t, shape):
        width = int(math.prod(shape))
        return jnp.stack([r[name][:, :width].reshape(n, t, *shape) for r in rows], axis=1)

    def stack_fm(name, feat, shape):
        a = jnp.stack([r[name][:, :feat, :] for r in rows_p], axis=1)
        return jnp.moveaxis(a.reshape(n_b, depth, *shape, seq), -1, 2)

    def stack_mem(i):
        return jnp.stack([r[i].reshape(n_b, MEM_TOKENS, M_HEADS, M_HEAD_DIM) for r in mem_rows_out], axis=1)

    kv_p = (stack_fm("kat32", LANES, (A_KV_HEADS, A_HEAD_DIM)), stack_fm("vat32", LANES, (A_KV_HEADS, A_HEAD_DIM)),
            stack_fm("ikt32", IDX_DIM, (IDX_DIM,)),
            stack(rows_p, "bk32", n_b, seq, (B_HEADS, 2 * B_QK_DIM)), stack(rows_p, "bv32", n_b, seq, (B_HEADS, B_V_DIM)))
    kv_s = (stack(rows_s, "ka32", n_s, dec_seq, (A_KV_HEADS, A_HEAD_DIM)),
            stack(rows_s, "va32", n_s, dec_seq, (A_KV_HEADS, A_HEAD_DIM)),
            stack(rows_s, "ik32", n_s, dec_seq, (IDX_DIM,)),
            stack(rows_s, "bk32", n_s, dec_seq, (B_HEADS, 2 * B_QK_DIM)),
            stack(rows_s, "bv32", n_s, dec_seq, (B_HEADS, B_V_DIM)))
    return ((yp.reshape(n_b, seq, d), ys.reshape(n_s, dec_seq, d)) + kv_p + (stack_mem(0), stack_mem(1)) + kv_s)
```

```python
import functools
import math

import jax
import jax.numpy as jnp
from jax import lax
from jax.experimental import pallas as pl
from jax.experimental.pallas import tpu as pltpu

F32 = jnp.float32
BF16 = jnp.bfloat16
I32 = jnp.int32

A_HEADS, A_KV_HEADS, A_HEAD_DIM = 8, 2, 64
IDX_HEADS, IDX_DIM, IDX_TOPK_MAX = 4, 64, 256
B_HEADS, B_QK_DIM, B_V_DIM = 4, 64, 128
MEM_TOKENS, M_HEADS, M_HEAD_DIM = 256, 4, 128
N_BRANCH = 3
ROPE_THETA = 10000.0
RMS_EPS = 1e-6

LANES = 128
V7X_VMEM_LIMIT_BYTES = 60000 * 1024

NEG_BIG = -1e30
INT_MIN = -(2 ** 31)
KEY_NEG_INF = -(2 ** 31) + 0x7FFFFF

PROJ_COLS = dict(aq=(0, 1024), ak=(1024, 128), av=(1152, 128), ik=(1280, 128), iq=(1408, 512),
                 bq=(1920, 1024), bk=(2944, 512), bv=(3456, 512))
PROJ_WIDTH = 3968


def _cparams(sem):
    return pltpu.CompilerParams(dimension_semantics=sem, vmem_limit_bytes=V7X_VMEM_LIMIT_BYTES)


def _resident(shape):
    zeros = (0,) * len(shape)
    return pl.BlockSpec(shape, lambda *_: zeros, pipeline_mode=pl.Buffered(1))


def _rms(x, g):
    return x * lax.rsqrt(jnp.mean(x * x, axis=-1, keepdims=True) + RMS_EPS) * g


def _dot(a, b):
    return jnp.dot(a, b, preferred_element_type=F32)


def _dot_nt(a, b):
    return lax.dot_general(a, b, (((1,), (1,)), ((), ())), preferred_element_type=F32)


def _row_tile(rows, target=512):
    t = min(rows, target)
    assert rows % t == 0 and t % 8 == 0, (rows, t)
    return t


def _ffn_body(*refs, d_ff, chunk, final):
    if final:
        x_ref, g_ref, wi_ref, wo_ref, fg_ref, o_ref, of_ref = refs
    else:
        x_ref, g_ref, wi_ref, wo_ref, o_ref = refs
    x = x_ref[...]
    h = _rms(x, g_ref[...]).astype(BF16)
    acc = jnp.zeros_like(x)
    for c in range(d_ff // chunk):
        gate = _dot(h, wi_ref[:, c * chunk:(c + 1) * chunk])
        up = _dot(h, wi_ref[:, d_ff + c * chunk:d_ff + (c + 1) * chunk])
        act = (gate * jax.nn.sigmoid(gate) * up).astype(BF16)
        acc = acc + _dot(act, wo_ref[c * chunk:(c + 1) * chunk, :])
    y = x + 0.5 * acc
    o_ref[...] = y
    if final:
        of_ref[...] = _rms(y, fg_ref[...])


def _ffn(x, g, w_in, w_out, final_g=None):
    rows, d = x.shape
    d_ff = w_out.shape[0]
    chunk = d_ff // 2 if (d_ff // 2) % LANES == 0 else d_ff
    tm = _row_tile(rows)
    final = final_g is not None
    row_spec = pl.BlockSpec((tm, d), lambda i: (i, 0))
    in_specs = [row_spec, _resident((1, d)), _resident(w_in.shape), _resident(w_out.shape)]
    args = [x, g.reshape(1, d), w_in, w_out]
    out_shape = [jax.ShapeDtypeStruct((rows, d), F32)]
    out_specs = [row_spec]
    if final:
        in_specs.append(_resident((1, d)))
        args.append(final_g.reshape(1, d))
        out_shape.append(jax.ShapeDtypeStruct((rows, d), F32))
        out_specs.append(row_spec)
    out = pl.pallas_call(
        functools.partial(_ffn_body, d_ff=d_ff, chunk=chunk, final=final),
        grid=(rows // tm,), in_specs=in_specs, out_specs=out_specs, out_shape=out_shape,
        compiler_params=_cparams(("arbitrary",)), name="ffn_final" if final else "ffn",
    )(*args)
    return out if final else out[0]


def _rope_tables(pos):
    half = A_HEAD_DIM // 2
    inv = ROPE_THETA ** (-jnp.arange(half, dtype=F32) / half)
    ang = pos.astype(F32)[:, None] * inv[None, :]
    cos, sin = jnp.cos(ang), jnp.sin(ang)
    cos = jnp.tile(cos, (1, LANES // half))
    sin = jnp.tile(jnp.concatenate([-sin, sin], axis=1), (1, LANES // (2 * half)))
    return cos, sin


def _proj_weight(wt):
    d = wt.shape[1]
    z64 = jnp.zeros((64, d), wt.dtype)
    o = 0
    aq = wt[o:o + 512]; o += 512
    ak = wt[o:o + 128]; o += 128
    av = wt[o:o + 128]; o += 128
    iq = wt[o:o + 256]; o += 256
    ik = wt[o:o + 64]; o += 64
    iw = wt[o:o + 4]; o += 4
    bq = wt[o:o + 512]; o += 512
    bk = wt[o:o + 512]; o += 512
    bv = wt[o:o + 512]; o += 512
    rows = []
    for h in range(A_HEADS):
        qh = aq[64 * h:64 * h + 64]
        rows += [qh, z64] if h // (A_HEADS // A_KV_HEADS) == 0 else [z64, qh]
    rows += [ak, av, ik, iw, jnp.zeros((60, d), wt.dtype)]
    for h in range(IDX_HEADS):
        rows += [iq[64 * h:64 * h + 64], z64]
    for j in range(2 * B_HEADS):
        qj = bq[64 * j:64 * j + 64]
        rows += [qj, z64] if j % 2 == 0 else [z64, qj]
    rows += [bk, bv]
    w = jnp.concatenate(rows, axis=0)
    assert w.shape[0] == PROJ_WIDTH
    return w.astype(BF16)


def _proj_body(x_ref, g_ref, w_ref, cos_ref, sin_ref, qa_ref, qi_ref, bq_ref, *kv_refs, prompt):
    if prompt:
        kat32_ref, vat32_ref, ikt32_ref, bk32_ref, bv32_ref, ka16_ref, ik16_ref, bk16_ref, vat16_ref, bvt16_ref = kv_refs
    else:
        ka32_ref, va32_ref, ik32_ref, bk32_ref, bv32_ref = kv_refs
    h = _rms(x_ref[...], g_ref[...]).astype(BF16)
    cos, sin = cos_ref[...], sin_ref[...]
    lane = lax.broadcasted_iota(I32, cos.shape, 1)
    first_half = (lane & 63) < 32

    def rope(y):
        rot = jnp.where(first_half, pltpu.roll(y, LANES - 32, 1), pltpu.roll(y, 32, 1))
        return y * cos + rot * sin

    def seg(name):
        c0, width = PROJ_COLS[name]
        return _dot_nt(h, w_ref[c0:c0 + width, :])

    def groups(y):
        return [y[:, LANES * j:LANES * (j + 1)] for j in range(y.shape[1] // LANES)]

    for j, y in enumerate(groups(seg("aq"))):
        qa_ref[:, LANES * j:LANES * (j + 1)] = (rope(y) * 0.125).astype(BF16)
    for j, y in enumerate(groups(seg("iq"))):
        qi_ref[:, LANES * j:LANES * (j + 1)] = (rope(y) * 0.125).astype(BF16)
    for j, y in enumerate(groups(seg("bq"))):
        bq_ref[:, LANES * j:LANES * (j + 1)] = (rope(y) * 0.125).astype(BF16)
    ka = rope(seg("ak"))
    va = seg("av")
    y = seg("ik")
    ik = jnp.where(lane < 64, rope(y), y)
    bk = [rope(y) for y in groups(seg("bk"))]
    bv = groups(seg("bv"))
    if not prompt:
        ka32_ref[...] = ka
        va32_ref[...] = va
        ik32_ref[...] = ik
        for j in range(B_HEADS):
            bk32_ref[:, LANES * j:LANES * (j + 1)] = bk[j]
            bv32_ref[:, LANES * j:LANES * (j + 1)] = bv[j]
        return
    tm = ka.shape[0]
    kat32_ref[...] = ka.T
    ka16_ref[...] = ka.astype(BF16)
    vat = va.T
    vat32_ref[...] = vat
    vat16_ref[...] = vat.astype(BF16)
    ikt32_ref[...] = ik.T
    ik16_ref[...] = ik.astype(BF16)
    for j in range(B_HEADS):
        rows = pl.ds(j, tm, stride=B_HEADS)
        bk32_ref[rows, :] = bk[j]
        bv32_ref[rows, :] = bv[j]
        bk16_ref[:, LANES * j:LANES * (j + 1)] = bk[j].astype(BF16)
        bvt16_ref[LANES * j:LANES * (j + 1), :] = bv[j].T.astype(BF16)


def _proj(x, g, w, cos, sin, *, prompt, n_batch=1):
    rows, d = x.shape
    seq = cos.shape[0]
    tm = _row_tile(seq)
    n_t = seq // tm
    row = lambda width: pl.BlockSpec((tm, width), lambda i: (i, 0))
    tab = pl.BlockSpec((tm, LANES), lambda i: (i % n_t, 0))
    rm = lambda width, dt: (row(width), jax.ShapeDtypeStruct((rows, width), dt))
    outs = dict(qa=rm(1024, BF16), qi=rm(512, BF16), bq=rm(1024, BF16))
    if prompt:
        fm = lambda feat, dt: (pl.BlockSpec((None, feat, tm), lambda i: (i // n_t, 0, i % n_t)),
                               jax.ShapeDtypeStruct((n_batch, feat, seq), dt))
        tiled = lambda feat, dt: (pl.BlockSpec((None, feat, tm), lambda i: (i, 0, 0)),
                                  jax.ShapeDtypeStruct((rows // tm, feat, tm), dt))
        head_rows = (pl.BlockSpec((tm * B_HEADS, LANES), lambda i: (i, 0)),
                     jax.ShapeDtypeStruct((rows * B_HEADS, LANES), F32))
        outs.update(kat32=fm(LANES, F32), vat32=fm(LANES, F32), ikt32=fm(LANES, F32), bk32=head_rows,
                    bv32=head_rows, ka16=rm(LANES, BF16), ik16=rm(LANES, BF16), bk16=rm(512, BF16),
                    vat16=tiled(LANES, BF16), bvt16=tiled(512, BF16))
    else:
        outs.update(ka32=rm(LANES, F32), va32=rm(LANES, F32), ik32=rm(LANES, F32), bk32=rm(512, F32),
                    bv32=rm(512, F32))
    out = pl.pallas_call(
        functools.partial(_proj_body, prompt=prompt), grid=(rows // tm,),
        in_specs=[row(d), _resident((1, d)), _resident(w.shape), tab, tab],
        out_specs=[spec for spec, _ in outs.values()],
        out_shape=[shape for _, shape in outs.values()],
        compiler_params=_cparams(("arbitrary",)), name="proj_prompt" if prompt else "proj_sample",
    )(x, g.reshape(1, d), w, cos, sin)
    return dict(zip(outs.keys(), out))


def _memkv_body(x_ref, g_ref, w_ref, k32_ref, v32_ref, k16_ref, v16_ref):
    h = _rms(x_ref[...], g_ref[...]).astype(BF16)
    y = _dot(h, w_ref[...])
    half = y.shape[1] // 2
    k, v = y[:, :half], y[:, half:]
    k32_ref[...] = k
    v32_ref[...] = v
    k16_ref[...] = k.astype(BF16)
    v16_ref[...] = v.astype(BF16)


def _memkv(x, g, w):
    rows, d = x.shape
    half = w.shape[1] // 2
    tm = _row_tile(rows)
    row = lambda width: pl.BlockSpec((tm, width), lambda i: (i, 0))
    return pl.pallas_call(
        _memkv_body, grid=(rows // tm,),
        in_specs=[row(d), _resident((1, d)), _resident(w.shape)],
        out_specs=[row(half)] * 4,
        out_shape=[jax.ShapeDtypeStruct((rows, half), dt) for dt in (F32, F32, BF16, BF16)],
        compiler_params=_cparams(("arbitrary",)), name="memkv",
    )(x, g.reshape(1, d), w)


def _float_key(s):
    bits = pltpu.bitcast(s + 0.0, I32)
    return bits ^ ((bits >> 31) & 0x7FFFFFFF)


def _kth_largest_key(count_ge, rows_shape, k):
    base = jnp.where(count_ge(jnp.zeros(rows_shape, I32)) >= k, 0, INT_MIN).astype(I32)

    def bit_step(i, base):
        cand = base | (jnp.int32(1) << (30 - i))
        return jnp.where(count_ge(cand) >= k, cand, base)

    return lax.fori_loop(0, 31, bit_step, base)


I16 = jnp.int16


def _pack16(x):
    half = x.shape[0] // 2
    return (x[:half] & 0xFFFF) | (x[half:] << 16)


def _kth_largest_16(count_ge, rows_shape, k):
    word = lambda c: (c & 0xFFFF) | (c << 16)
    base = jnp.where(count_ge(jnp.zeros(rows_shape, I32)) >= k, 0, -(2 ** 15)).astype(I32)

    def bit_step(i, base):
        cand = base | (jnp.int32(1) << (14 - i))
        return jnp.where(count_ge(word(cand)) >= k, cand, base)

    return lax.fori_loop(0, 15, bit_step, base)


def _softmax_step(s, m_prev, l_prev, axis=-1):
    m_new = jnp.maximum(m_prev, jnp.max(s, axis=axis, keepdims=True))
    alpha = jnp.exp(m_prev - m_new)
    p = jnp.exp(s - m_new)
    l_new = alpha * l_prev + jnp.sum(p, axis=axis, keepdims=True)
    return p, m_new, l_new, alpha


def _dsa_body(qa_ref, qi_ref, ikt_ref, ka_ref, ik_ref, vat_ref, tril_ref, o_ref, key_scr, half_scr, acc_scr,
              s_scr, *, tq, tk, top_k):
    q0 = pl.program_id(1) * tq
    n_kb = (q0 + tq + tk - 1) // tk
    rep = A_HEADS // A_KV_HEADS
    q_pos = q0 + lax.broadcasted_iota(I32, (1, tq), 1)
    k_iota = lax.broadcasted_iota(I32, (tk, 1), 0)

    qi = jnp.concatenate([qi_ref[:, LANES * h:LANES * (h + 1)] for h in range(IDX_HEADS)], axis=0)
    w_heads = [ikt_ref[IDX_DIM + h:IDX_DIM + h + 1, :] for h in range(IDX_HEADS)]

    def score_block(kb, carry):
        k0 = pl.multiple_of(kb * tk, tk)
        logits = _dot_nt(ik_ref[pl.ds(k0, tk), :], qi)
        s = jnp.zeros((tk, tq), F32)
        for h in range(IDX_HEADS):
            s = s + jnp.maximum(logits[:, h * tq:(h + 1) * tq], 0.0) * w_heads[h]
        s = jnp.where(k0 + k_iota <= q_pos, s, -jnp.inf)
        key = _float_key(s)
        key_scr[kb] = key
        half_scr[kb] = _pack16(key >> 16)
        return carry

    lax.fori_loop(0, n_kb, score_block, 0)

    def count_ge(cand):
        def blk(kb, acc):
            c = jnp.where(key_scr[kb] >= cand, 1, 0)
            return acc + jnp.sum(c.reshape(tk // 8, 8, tq), axis=0)
        acc = lax.fori_loop(0, n_kb, blk, jnp.zeros((8, tq), I32))
        return jnp.sum(acc, axis=0, keepdims=True)

    def count_ge_16(cand_word):
        cand = pltpu.bitcast(jnp.broadcast_to(cand_word, (8, tq)), I16)
        def blk(kb, acc):
            v = pltpu.bitcast(half_scr[kb], I16)
            ones = [jnp.where(v[16 * i:16 * (i + 1)] >= cand, jnp.int16(1), jnp.int16(0)) for i in range(tk // 16)]
            while len(ones) > 1:
                ones = [a + b for a, b in zip(ones[0::2], ones[1::2])]
            return acc + ones[0]
        acc = pltpu.bitcast(lax.fori_loop(0, n_kb, blk, jnp.zeros((16, tq), I16)), I32)
        return jnp.sum((acc & 0xFFFF) + (acc >> 16), axis=0, keepdims=True)

    thr_hi = _kth_largest_16(count_ge_16, (1, tq), top_k)

    def low_halves(kb, carry):
        key = key_scr[kb]
        hi = key >> 16
        low = (key & 0xFFFF) - 2 ** 15
        half_scr[kb] = _pack16(jnp.where(hi == thr_hi, low, jnp.where(hi > thr_hi, 2 ** 15 - 1, -(2 ** 15))))
        return carry

    lax.fori_loop(0, n_kb, low_halves, 0)
    thr_lo = _kth_largest_16(count_ge_16, (1, tq), top_k)
    thr = (thr_hi << 16) | ((thr_lo + 2 ** 15) & 0xFFFF)
    need = (top_k - count_ge(thr + 1)).astype(F32)
    thr_tie = jnp.where(thr == KEY_NEG_INF, jnp.int32(2 ** 31 - 1), thr)
    thr_all = jnp.where(thr == KEY_NEG_INF, KEY_NEG_INF + 1, thr)

    q_groups = [jnp.concatenate([qa_ref[:, LANES * (g * rep + r):LANES * (g * rep + r + 1)] for r in range(rep)], axis=0)
                for g in range(A_KV_HEADS)]
    acc_scr[...] = jnp.zeros(acc_scr.shape, F32)

    def logits(kb, g):
        k0 = pl.multiple_of(kb * tk, tk)
        s_scr[g] = _dot_nt(ka_ref[pl.ds(k0, tk), :], q_groups[g])

    def mask_bias(kb, tie_seen, rank_ties):
        key = key_scr[kb]
        if not rank_ties:
            return jnp.where(key >= thr_all, 0.0, NEG_BIG), tie_seen
        tie = key == thr_tie
        rank = tie_seen + _dot(tril_ref[...], jnp.where(tie, 1.0, 0.0).astype(BF16))
        bias = jnp.where(key > thr, 0.0, jnp.where(tie, jnp.where(rank <= need, 0.0, NEG_BIG), NEG_BIG))
        return bias, rank[tk - 1:tk, :]

    def attend(kb, g, bias, m, l):
        s = s_scr[g]
        s = jnp.concatenate([s[:, r * tq:(r + 1) * tq] + bias for r in range(rep)], axis=1)
        p, m, l, alpha = _softmax_step(s, m, l, axis=0)
        v_g = vat_ref[kb, A_HEAD_DIM * g:A_HEAD_DIM * (g + 1), :]
        acc_scr[g] = alpha * acc_scr[g] + _dot(v_g, p.astype(BF16))
        return m, l

    def block(kb, carry, more_blocks, rank_ties):
        m0, l0, m1, l1, tie_seen = carry
        logits(kb, 1)
        bias, tie_seen = mask_bias(kb, tie_seen, rank_ties)
        m0, l0 = attend(kb, 0, bias, m0, l0)
        if more_blocks:
            logits(kb + 1, 0)
        m1, l1 = attend(kb, 1, bias, m1, l1)
        return m0, l0, m1, l1, tie_seen

    def attend_all(rank_ties):
        stats = (1, rep * tq)
        neg, zero = jnp.full(stats, NEG_BIG, F32), jnp.zeros(stats, F32)
        logits(0, 0)
        carry = lax.fori_loop(0, n_kb - 1, lambda kb, c: block(kb, c, True, rank_ties),
                              (neg, zero, neg, zero, jnp.zeros((1, tq), F32)))
        _, l0, _, l1, _ = block(n_kb - 1, carry, False, rank_ties)
        return l0, l1

    excess = jnp.where(thr == KEY_NEG_INF, 0, count_ge(thr) - top_k)
    l_groups = lax.cond(jnp.max(excess.astype(F32)) > 0.0, lambda: attend_all(True), lambda: attend_all(False))

    for g in range(A_KV_HEADS):
        o = acc_scr[g] / l_groups[g]
        for j in range(rep // 2):
            pair = jnp.concatenate([o[:, (2 * j) * tq:(2 * j + 1) * tq], o[:, (2 * j + 1) * tq:(2 * j + 2) * tq]], axis=0)
            c = g * (rep // 2) + j
            o_ref[:, LANES * c:LANES * (c + 1)] = pair.T.astype(BF16)


def _dsa_prompt(p, n_batch, seq, top_k):
    tq = min(128, seq)
    tk = p["vat16"].shape[2]
    rep = A_HEADS // A_KV_HEADS
    n_q = seq // tq
    tril = jnp.tril(jnp.ones((tk, tk), BF16))
    qrow = lambda width: pl.BlockSpec((tq, width), lambda b, i: (b * n_q + i, 0))
    seqblk = lambda width: pl.BlockSpec((seq, width), lambda b, i: (b, 0))
    return pl.pallas_call(
        functools.partial(_dsa_body, tq=tq, tk=tk, top_k=top_k),
        grid=(n_batch, n_q),
        in_specs=[qrow(1024), qrow(512), pl.BlockSpec((None, LANES, tq), lambda b, i: (b, 0, i)),
                  seqblk(LANES), seqblk(LANES), pl.BlockSpec((seq // tk, LANES, tk), lambda b, i: (b, 0, 0)),
                  _resident((tk, tk))],
        out_specs=qrow(512),
        out_shape=jax.ShapeDtypeStruct((n_batch * seq, 512), BF16),
        scratch_shapes=[pltpu.VMEM((seq // tk, tk, tq), I32),
                        pltpu.VMEM((seq // tk, tk // 2, tq), I32),
                        pltpu.VMEM((A_KV_HEADS, A_HEAD_DIM, rep * tq), F32),
                        pltpu.VMEM((A_KV_HEADS, tk, rep * tq), F32)],
        compiler_params=_cparams(("arbitrary", "arbitrary")), name="dsa_prompt",
    )(p["qa"], p["qi"], p["ikt32"], p["ka16"], p["ik16"], p["vat16"], tril)


def _diff_lambda(lq1, lk1, lq2, lk2, lambda_init):
    s1 = jnp.sum(lq1 * lk1, axis=-1, keepdims=True)
    s2 = jnp.sum(lq2 * lk2, axis=-1, keepdims=True)
    return jnp.exp(s1) - jnp.exp(s2) + lambda_init


def _subln(o, gain, lambda_init):
    return _rms(o, gain) * (1.0 - lambda_init)


def _diff_body(bq_ref, bk_ref, bvt_ref, lq1_ref, lk1_ref, lq2_ref, lk2_ref, gain_ref, o_ref, acc_scr, s_scr,
               *, tq, tk, lambda_init):
    q0 = pl.program_id(1) * tq
    n_full = q0 // tk
    assert tq <= tk
    qp = q0 + lax.broadcasted_iota(I32, (1, tq), 1)
    q_pos = jnp.concatenate([qp, qp], axis=1)
    k_iota = lax.broadcasted_iota(I32, (tk, 1), 0)
    acc_scr[...] = jnp.zeros(acc_scr.shape, F32)
    q_heads = [jnp.concatenate([bq_ref[:, LANES * (2 * h + c):LANES * (2 * h + c + 1)] for c in range(2)], axis=0)
               for h in range(B_HEADS)]

    def logits(kb, h):
        k0 = pl.multiple_of(kb * tk, tk)
        s_scr[h % 2] = _dot_nt(bk_ref[pl.ds(k0, tk), LANES * h:LANES * (h + 1)], q_heads[h])

    def attend(kb, h, m, l, masked):
        s = s_scr[h % 2]
        if masked:
            s = jnp.where(kb * tk + k_iota <= q_pos, s, NEG_BIG)
        p, m, l, alpha = _softmax_step(s, m, l, axis=0)
        vt = bvt_ref[kb, LANES * h:LANES * (h + 1), :]
        acc_scr[h] = alpha * acc_scr[h] + _dot(vt, p.astype(BF16))
        return m, l

    def block(kb, stats, masked):
        out = []
        for h in range(B_HEADS):
            if h + 1 < B_HEADS:
                logits(kb, h + 1)
            elif not masked:
                logits(kb + 1, 0)
            out += attend(kb, h, stats[2 * h], stats[2 * h + 1], masked)
        return tuple(out)

    shape = (1, 2 * tq)
    init = tuple(jnp.full(shape, NEG_BIG, F32) if i % 2 == 0 else jnp.zeros(shape, F32) for i in range(2 * B_HEADS))
    logits(0, 0)
    stats = lax.fori_loop(0, n_full, lambda kb, st: block(kb, st, False), init)
    stats = block(n_full, stats, True)

    lam = _diff_lambda(lq1_ref[...], lk1_ref[...], lq2_ref[...], lk2_ref[...], lambda_init)
    for h in range(B_HEADS):
        o = acc_scr[h] / stats[2 * h + 1]
        o = (o[:, :tq] - lam * o[:, tq:]).T
        o_ref[:, LANES * h:LANES * (h + 1)] = _subln(o, gain_ref[...], lambda_init).astype(BF16)


def _diff_prompt(p, lam_params, gain, n_batch, seq, lambda_init):
    tk = p["bvt16"].shape[2]
    tq = min(256, seq)
    assert tk % tq == 0 or tq % tk == 0
    n_q = seq // tq
    qrow = lambda width: pl.BlockSpec((tq, width), lambda b, i: (b * n_q + i, 0))
    small = [_resident((1, a.shape[-1])) for a in lam_params] + [_resident((1, B_V_DIM))]
    return pl.pallas_call(
        functools.partial(_diff_body, tq=tq, tk=tk, lambda_init=lambda_init),
        grid=(n_batch, n_q),
        in_specs=[qrow(1024), pl.BlockSpec((seq, 512), lambda b, i: (b, 0)),
                  pl.BlockSpec((seq // tk, 512, tk), lambda b, i: (b, 0, 0))] + small,
        out_specs=qrow(512),
        out_shape=jax.ShapeDtypeStruct((n_batch * seq, 512), BF16),
        scratch_shapes=[pltpu.VMEM((B_HEADS, B_V_DIM, 2 * tq), F32), pltpu.VMEM((2, tk, 2 * tq), F32)],
        compiler_params=_cparams(("arbitrary", "arbitrary")), name="diff_prompt",
    )(p["bq"], p["bk16"], p["bvt16"], *[a.reshape(1, -1) for a in lam_params], gain.reshape(1, -1))


def _post_body(*refs, mem_in_kernel):
    if mem_in_kernel:
        (x_ref, g_ref, oa_ref, ob_ref, mk_ref, mv_ref, wmq_ref, wg_ref, wpa_ref, wpb_ref, wpm_ref, wo_ref,
         o_ref) = refs
    else:
        x_ref, g_ref, oa_ref, ob_ref, om_ref, wg_ref, wpa_ref, wpb_ref, wpm_ref, wo_ref, o_ref = refs
    x = x_ref[...]
    d = x.shape[1]
    h = _rms(x, g_ref[...]).astype(BF16)
    if mem_in_kernel:
        mq = (_dot_nt(h, wmq_ref[...]) * (M_HEAD_DIM ** -0.5)).astype(BF16)
        heads = []
        for hd in range(M_HEADS):
            sl = slice(LANES * hd, LANES * (hd + 1))
            s = _dot_nt(mq[:, sl], mk_ref[:, sl])
            e = jnp.exp(s - jnp.max(s, axis=-1, keepdims=True))
            o = _dot(e.astype(BF16), mv_ref[:, sl]) / jnp.sum(e, axis=-1, keepdims=True)
            heads.append(o.astype(BF16))
        om = jnp.concatenate(heads, axis=1)
    else:
        om = om_ref[...]

    def gate(i):
        return jax.nn.sigmoid(_dot_nt(h, wg_ref[d * i:d * (i + 1), :]))

    merged = gate(0) * _dot(oa_ref[...], wpa_ref[...])
    merged = merged + gate(1) * _dot(ob_ref[...], wpb_ref[...])
    merged = merged + gate(2) * _dot(om, wpm_ref[...])
    o_ref[...] = x + _dot(merged.astype(BF16), wo_ref[...])


def _post(x, g, oa, ob, w, *, mem=None, om=None, n_batch=1):
    rows, d = x.shape
    seq = rows // n_batch
    tm = _row_tile(seq)
    n_t = seq // tm
    row = lambda width: pl.BlockSpec((tm, width), lambda b, i: (b * n_t + i, 0))
    weights = [w["gates"], w["proj_a"], w["proj_b"], w["proj_m"], w["out"]]
    if mem is not None:
        mk, mv, w_mq = mem
        memblk = pl.BlockSpec((MEM_TOKENS, mk.shape[1]), lambda b, i: (b, 0))
        args = [x, g.reshape(1, d), oa, ob, mk, mv, w_mq] + weights
        in_specs = ([row(d), _resident((1, d)), row(512), row(512), memblk, memblk, _resident(w_mq.shape)]
                    + [_resident(a.shape) for a in weights])
    else:
        args = [x, g.reshape(1, d), oa, ob, om] + weights
        in_specs = [row(d), _resident((1, d)), row(512), row(512), row(512)] + [_resident(a.shape) for a in weights]
    return pl.pallas_call(
        functools.partial(_post_body, mem_in_kernel=mem is not None),
        grid=(n_batch, n_t), in_specs=in_specs, out_specs=row(d),
        out_shape=jax.ShapeDtypeStruct((rows, d), F32),
        compiler_params=_cparams(("arbitrary", "arbitrary")), name="post",
    )(*args)


def _mem_sample_body(x_ref, g_ref, wmq_ref, mk_ref, mv_ref, o_ref, mq_scr):
    b = pl.program_id(0)

    @pl.when(b == 0)
    def _():
        h = _rms(x_ref[...], g_ref[...]).astype(BF16)
        mq_scr[...] = _dot_nt(h, wmq_ref[...]) * (M_HEAD_DIM ** -0.5)

    q = mq_scr[pl.ds(b, 1), :]
    for hd in range(M_HEADS):
        sl = slice(LANES * hd, LANES * (hd + 1))
        rows = pl.ds(hd, MEM_TOKENS, stride=M_HEADS)
        s = jnp.sum(mk_ref[rows, :] * q[:, sl], axis=-1, keepdims=True)
        e = jnp.exp(s - jnp.max(s, axis=0, keepdims=True))
        o = jnp.sum(e * mv_ref[rows, :], axis=0, keepdims=True) / jnp.sum(e, axis=0, keepdims=True)
        o_ref[:, sl] = o.astype(BF16)


def _mem_sample(x, g, w_mq, cache_k, cache_v, layer):
    n, d = x.shape
    width = w_mq.shape[0]
    memblk = pl.BlockSpec((None, None) + cache_k.shape[2:], lambda b: (b, layer, 0, 0))
    return pl.pallas_call(
        _mem_sample_body, grid=(n,),
        in_specs=[_resident((n, d)), _resident((1, d)), _resident(w_mq.shape), memblk, memblk],
        out_specs=pl.BlockSpec((None, 1, width), lambda b: (b, 0, 0)),
        out_shape=jax.ShapeDtypeStruct((n, 1, width), BF16),
        scratch_shapes=[pltpu.VMEM((n, width), F32)],
        compiler_params=_cparams(("arbitrary",)), name="mem_sample",
    )(x, g.reshape(1, d), w_mq, cache_k, cache_v)


def _page_copies(pt_ref, b, chunk, pages_per_chunk, layer, streams):
    slot = chunk % 2
    out = []
    for cache, dst, sem in streams:
        for j in range(pages_per_chunk):
            page = pt_ref[b, chunk * pages_per_chunk + j]
            out.append(pltpu.make_async_copy(cache.at[page, layer], dst(slot, j), sem.at[slot]))
    return out


def _lane_pages(buf, page):
    return lambda slot, j: buf.at[slot, :, pl.ds(page * j, page)]


def _row_pages(buf, rows):
    return lambda slot, j: buf.at[slot, pl.ds(rows * j, rows), :]


def _chunks(n_pages):
    ppc = min(16, n_pages)
    assert n_pages % ppc == 0
    return ppc, n_pages // ppc


def _for_each_chunk(pt_ref, n_pages, layer, streams, body):
    b = pl.program_id(0)
    n_samples = pl.num_programs(0)
    ppc, n_ch = _chunks(n_pages)
    across_samples = n_ch % 2 == 0

    def start(sample, c):
        for cp in _page_copies(pt_ref, sample, c, ppc, layer, streams):
            cp.start()

    if across_samples:
        pl.when(b == 0)(lambda: start(b, 0))
    else:
        start(b, 0)
    for c in range(n_ch):
        if c + 1 < n_ch:
            start(b, c + 1)
        elif across_samples:
            pl.when(b + 1 < n_samples)(lambda: start(b + 1, 0))
        for cp in _page_copies(pt_ref, b, c, ppc, layer, streams):
            cp.wait()
        body(c)


def _idx_sample_body(pt_ref, qi_ref, w_ref, ikn_ref, cache_ref, s_ref, buf, sem, *, layer, n_pages, page):
    ppc, n_ch = _chunks(n_pages)
    ck = ppc * page
    q = qi_ref[:, 0:IDX_DIM]
    w = w_ref[...]

    def scores(logits):
        return jnp.sum(jnp.maximum(logits, 0.0) * w, axis=0, keepdims=True)

    def chunk(c):
        s_ref[:, ck * c:ck * (c + 1)] = scores(_dot(q, buf[c % 2].astype(BF16)))

    _for_each_chunk(pt_ref, n_pages, layer, [(cache_ref, _lane_pages(buf, page), sem)], chunk)
    lane = lax.broadcasted_iota(I32, (1, LANES), 1)
    new_key = ikn_ref[:, 0:IDX_DIM].astype(BF16).astype(F32)
    s_new = scores(jnp.sum(q.astype(F32) * new_key, axis=1, keepdims=True))
    s_ref[:, ck * n_ch:] = jnp.where(lane == 0, s_new, -jnp.inf)


def _idx_sample(page_table, qi8, w8, ik_new, cache_t, layer):
    n, n_pages = page_table.shape
    feat, page = cache_t.shape[2], cache_t.shape[3]
    ppc, n_ch = _chunks(n_pages)
    nk = n_pages * page + LANES
    grid_spec = pltpu.PrefetchScalarGridSpec(
        num_scalar_prefetch=1, grid=(n,),
        in_specs=[pl.BlockSpec((None, 8, LANES), lambda b, pt: (b, 0, 0)),
                  pl.BlockSpec((None, 8, 1), lambda b, pt: (b, 0, 0)),
                  pl.BlockSpec((None, 1, LANES), lambda b, pt: (b, 0, 0)),
                  pl.BlockSpec(memory_space=pl.ANY)],
        out_specs=pl.BlockSpec((None, 1, nk), lambda b, pt: (b, 0, 0)),
        scratch_shapes=[pltpu.VMEM((2, feat, ppc * page), F32), pltpu.SemaphoreType.DMA((2,))])
    return pl.pallas_call(
        functools.partial(_idx_sample_body, layer=layer, n_pages=n_pages, page=page),
        grid_spec=grid_spec, out_shape=jax.ShapeDtypeStruct((n, 1, nk), F32),
        compiler_params=_cparams(("arbitrary",)), name="idx_sample",
    )(page_table, qi8, w8, ik_new, cache_t)


def _select_body(s_ref, bias_ref, *, top_k, n_valid):
    key = _float_key(s_ref[...])
    rows, nk = key.shape
    shape = (rows, 1)

    def count(ones):
        return jnp.sum(ones, axis=1, keepdims=True)

    thr = _kth_largest_key(lambda cand: count(jnp.where(key >= cand, 1, 0)), shape, top_k)
    need = top_k - count(jnp.where(key > thr, 1, 0))
    thr_tie = jnp.where(thr == KEY_NEG_INF, jnp.int32(2 ** 31 - 1), thr)
    tie = jnp.where(key == thr_tie, 1, 0)
    idx = lax.broadcasted_iota(I32, key.shape, 1)
    n_bits = max(1, (nk - 1).bit_length())

    def bit_step(i, last):
        cand = last - (jnp.int32(1) << (n_bits - 1 - i))
        return jnp.where(count(jnp.where(idx <= cand, tie, 0)) >= need, cand, last)

    last = lax.fori_loop(0, n_bits, bit_step, jnp.full(shape, 2 ** n_bits - 1, I32))
    tie_bias = jnp.where(idx <= last, jnp.where(tie > 0, 0.0, NEG_BIG), NEG_BIG)
    bias = jnp.where(key > thr, 0.0, tie_bias)
    bias_ref[...] = jnp.where(idx < n_valid, bias, NEG_BIG)


def _select(scores, top_k, n_valid):
    return pl.pallas_call(
        functools.partial(_select_body, top_k=top_k, n_valid=n_valid),
        out_shape=jax.ShapeDtypeStruct(scores.shape, F32),
        compiler_params=pltpu.CompilerParams(vmem_limit_bytes=V7X_VMEM_LIMIT_BYTES), name="select_sample",
    )(scores)


def _bf16_round(x):
    return x.astype(BF16).astype(F32)


def _new_row_step(s_new, v_new, m, l, acc):
    m_new = jnp.maximum(m, s_new)
    alpha = jnp.exp(m - m_new)
    p_new = jnp.exp(s_new - m_new)
    return alpha * l + p_new, alpha * acc + p_new * v_new


def _dsa_sample_body(pt_ref, q_ref, kn_ref, vn_ref, bias_ref, kc_ref, vc_ref, o_ref, kbuf, vbuf, ksem, vsem,
                     *, layer, n_pages, page):
    ppc, n_ch = _chunks(n_pages)
    ck = ppc * page
    q = q_ref[...]
    st = dict(m=jnp.full((8, 1), NEG_BIG, F32), l=jnp.zeros((8, 1), F32), acc=jnp.zeros((8, LANES), F32))

    def chunk(c):
        s = _dot(q, kbuf[c % 2].astype(BF16)) + bias_ref[:, ck * c:ck * (c + 1)]
        p, st["m"], st["l"], alpha = _softmax_step(s, st["m"], st["l"])
        st["acc"] = alpha * st["acc"] + _dot_nt(p.astype(BF16), vbuf[c % 2].astype(BF16))

    _for_each_chunk(pt_ref, n_pages, layer,
                    [(kc_ref, _lane_pages(kbuf, page), ksem), (vc_ref, _lane_pages(vbuf, page), vsem)], chunk)
    s_new = (jnp.sum(q.astype(F32) * _bf16_round(kn_ref[...]), axis=1, keepdims=True)
             + bias_ref[:, ck * n_ch:ck * n_ch + 1])
    l, acc = _new_row_step(s_new, _bf16_round(vn_ref[...]), st["m"], st["l"], st["acc"])
    o_ref[...] = acc / l


def _diff_sample_body(pt_ref, q_ref, kn_ref, vn_ref, kc_ref, vc_ref, lq1_ref, lk1_ref, lq2_ref, lk2_ref,
                      gain_ref, o_ref, kbuf, vbuf, ksem, vsem, *, layer, n_pages, page, lambda_init):
    ppc, n_ch = _chunks(n_pages)
    ck = ppc * page
    page_rows = page * B_HEADS
    q = q_ref[...]
    row_head = lax.broadcasted_iota(I32, (8, 1), 0) >> 1
    st = dict(m=jnp.full((8, 1), NEG_BIG, F32), l=jnp.zeros((8, 1), F32), acc=jnp.zeros((8, B_HEADS * LANES), F32))

    def chunk(c):
        head_rows = lambda buf, h: buf[c % 2, pl.ds(h, ck, stride=B_HEADS), :].astype(BF16)
        s = jnp.zeros((8, ck), F32)
        for h in range(B_HEADS):
            s = jnp.where(row_head == h, _dot_nt(q, head_rows(kbuf, h)), s)
        p, st["m"], st["l"], alpha = _softmax_step(s, st["m"], st["l"])
        pb = p.astype(BF16)
        pv = jnp.concatenate([_dot(pb, head_rows(vbuf, h)) for h in range(B_HEADS)], axis=1)
        st["acc"] = alpha * st["acc"] + pv

    _for_each_chunk(pt_ref, n_pages, layer,
                    [(kc_ref, _row_pages(kbuf, page_rows), ksem), (vc_ref, _row_pages(vbuf, page_rows), vsem)], chunk)
    qf = q.astype(F32)
    kn = _bf16_round(kn_ref[...])
    s_new = jnp.zeros((8, 1), F32)
    for h in range(B_HEADS):
        s_h = jnp.sum(qf * kn[:, LANES * h:LANES * (h + 1)], axis=1, keepdims=True)
        s_new = jnp.where(row_head == h, s_h, s_new)
    l, acc = _new_row_step(s_new, _bf16_round(vn_ref[...]), st["m"], st["l"], st["acc"])
    o = acc / l
    lam = _diff_lambda(lq1_ref[...], lk1_ref[...], lq2_ref[...], lk2_ref[...], lambda_init)
    for h in range(B_HEADS):
        sl = slice(LANES * h, LANES * (h + 1))
        oh = o[2 * h:2 * h + 1, sl] - lam * o[2 * h + 1:2 * h + 2, sl]
        o_ref[:, sl] = _subln(oh, gain_ref[...], lambda_init).astype(BF16)


def _per_sample(a):
    return pl.BlockSpec((None,) + a.shape[1:], lambda b, pt: (b,) + (0,) * (a.ndim - 1))


def _dsa_sample(page_table, q8, k_new, v_new, bias, cache_kt, cache_vt, layer):
    n, n_pages = page_table.shape
    feat, page = cache_kt.shape[2], cache_kt.shape[3]
    ppc, _ = _chunks(n_pages)
    any_spec = pl.BlockSpec(memory_space=pl.ANY)
    grid_spec = pltpu.PrefetchScalarGridSpec(
        num_scalar_prefetch=1, grid=(n,),
        in_specs=[_per_sample(q8), _per_sample(k_new), _per_sample(v_new), _per_sample(bias), any_spec, any_spec],
        out_specs=pl.BlockSpec((None, 8, feat), lambda b, pt: (b, 0, 0)),
        scratch_shapes=[pltpu.VMEM((2, feat, ppc * page), F32), pltpu.VMEM((2, feat, ppc * page), F32),
                        pltpu.SemaphoreType.DMA((2,)), pltpu.SemaphoreType.DMA((2,))])
    return pl.pallas_call(
        functools.partial(_dsa_sample_body, layer=layer, n_pages=n_pages, page=page),
        grid_spec=grid_spec, out_shape=jax.ShapeDtypeStruct((n, 8, feat), F32),
        compiler_params=_cparams(("arbitrary",)), name="dsa_sample",
    )(page_table, q8, k_new, v_new, bias, cache_kt, cache_vt)


def _diff_sample(page_table, q8, k_new, v_new, cache_k, cache_v, layer, diff_params, lambda_init):
    n, n_pages = page_table.shape
    page_rows = cache_k.shape[2]
    ppc, _ = _chunks(n_pages)
    width = B_HEADS * B_V_DIM
    any_spec = pl.BlockSpec(memory_space=pl.ANY)
    extra = [a.reshape(1, -1) for a in diff_params]
    grid_spec = pltpu.PrefetchScalarGridSpec(
        num_scalar_prefetch=1, grid=(n,),
        in_specs=([_per_sample(q8), _per_sample(k_new), _per_sample(v_new), any_spec, any_spec]
                  + [pl.BlockSpec(a.shape, lambda b, pt: (0, 0)) for a in extra]),
        out_specs=pl.BlockSpec((None, 1, width), lambda b, pt: (b, 0, 0)),
        scratch_shapes=[pltpu.VMEM((2, ppc * page_rows, LANES), F32), pltpu.VMEM((2, ppc * page_rows, LANES), F32),
                        pltpu.SemaphoreType.DMA((2,)), pltpu.SemaphoreType.DMA((2,))])
    return pl.pallas_call(
        functools.partial(_diff_sample_body, layer=layer, n_pages=n_pages, page=page_rows // B_HEADS,
                          lambda_init=lambda_init),
        grid_spec=grid_spec, out_shape=jax.ShapeDtypeStruct((n, 1, width), BF16),
        compiler_params=_cparams(("arbitrary",)), name="diff_sample",
    )(page_table, q8, k_new, v_new, cache_k, cache_v, *extra)


def _layer_weights(l, w_in_t, w_proj_a, w_proj_b, w_proj_m, w_out):
    o_mq = sum((512, 128, 128, 256, 64, 4, 512, 512, 512))
    wt = w_in_t[:, l, :]
    return dict(proj=_proj_weight(wt),
                mq=wt[o_mq:o_mq + 512].astype(BF16),
                gates=wt[o_mq + 512:].astype(BF16),
                proj_a=w_proj_a[l].astype(BF16), proj_b=w_proj_b[l].astype(BF16),
                proj_m=w_proj_m[l].astype(BF16), out=w_out[l].astype(BF16))


def kernel(x_prompt, x_sample, cache_a_k, cache_a_v, cache_idx_k, cache_b_k, cache_b_v, cache_mem_k, cache_mem_v, page_table, mem_prompt, norm_ffn1, ffn1_w_in, ffn1_w_out, norm_mix, w_in, lambda_q1, lambda_k1, lambda_q2, lambda_k2, subln_gain, norm_mem, w_mem_kv, w_proj_a, w_proj_b, w_proj_m, w_out, norm_ffn2, ffn2_w_in, ffn2_w_out, final_norm):
    n_b, seq, d = x_prompt.shape
    n_s, dec_seq, _ = x_sample.shape
    assert dec_seq == 1
    depth = norm_ffn1.shape[0]
    n_pool, _, page = cache_a_k.shape[:3]
    n_pages = page_table.shape[1]
    past_len = n_pages * page
    top_k_prompt = min(IDX_TOPK_MAX, seq // 4)
    top_k_sample = min(IDX_TOPK_MAX, (past_len + dec_seq) // 4)

    cos_p, sin_p = _rope_tables(jnp.arange(seq, dtype=I32))
    cos_s, sin_s = _rope_tables(jnp.full((n_s,), past_len, I32))
    c_ik = jnp.transpose(cache_idx_k, (0, 1, 3, 2))
    c_ak = jnp.transpose(cache_a_k, (0, 1, 3, 4, 2)).reshape(n_pool, depth, -1, page)
    c_av = jnp.transpose(cache_a_v, (0, 1, 3, 4, 2)).reshape(n_pool, depth, -1, page)
    c_bk = cache_b_k.reshape(n_pool, depth, page * B_HEADS, -1)
    c_bv = cache_b_v.reshape(n_pool, depth, page * B_HEADS, -1)
    c_mk = cache_mem_k.reshape(n_s, depth, MEM_TOKENS * M_HEADS, -1)
    c_mv = cache_mem_v.reshape(n_s, depth, MEM_TOKENS * M_HEADS, -1)
    mem_rows = mem_prompt.reshape(n_b * MEM_TOKENS, d)
    w_in_t = jnp.transpose(w_in, (2, 0, 1))

    xp = x_prompt.reshape(n_b * seq, d)
    xs = x_sample.reshape(n_s, d)
    rows_p, rows_s, mem_rows_out = [], [], []
    yp = ys = None
    for l in range(depth):
        last = l == depth - 1
        lambda_init = 0.8 - 0.6 * math.exp(-0.3 * l)
        w = _layer_weights(l, w_in_t, w_proj_a, w_proj_b, w_proj_m, w_out)
        f1_in, f1_out = ffn1_w_in[l].astype(BF16), ffn1_w_out[l].astype(BF16)
        f2_in, f2_out = ffn2_w_in[l].astype(BF16), ffn2_w_out[l].astype(BF16)
        lam_params = (lambda_q1[l], lambda_k1[l], lambda_q2[l], lambda_k2[l])

        mk32, mv32, mk16, mv16 = _memkv(mem_rows, norm_mem[l], w_mem_kv[l].astype(BF16))
        mem_rows_out.append((mk32, mv32))
        xp = _ffn(xp, norm_ffn1[l], f1_in, f1_out)
        p = _proj(xp, norm_mix[l], w["proj"], cos_p, sin_p, prompt=True, n_batch=n_b)
        rows_p.append(p)
        oa = _dsa_prompt(p, n_b, seq, top_k_prompt)
        ob = _diff_prompt(p, lam_params, subln_gain[l], n_b, seq, lambda_init)
        xp = _post(xp, norm_mix[l], oa, ob, w, mem=(mk16, mv16, w["mq"]), n_batch=n_b)
        if last:
            xp, yp = _ffn(xp, norm_ffn2[l], f2_in, f2_out, final_norm)
        else:
            xp = _ffn(xp, norm_ffn2[l], f2_in, f2_out)

        xs = _ffn(xs, norm_ffn1[l], f1_in, f1_out)
        s = _proj(xs, norm_mix[l], w["proj"], cos_s, sin_s, prompt=False)
        rows_s.append(s)
        qi8 = jnp.pad(s["qi"].reshape(n_s, IDX_HEADS, LANES), ((0, 0), (0, 8 - IDX_HEADS), (0, 0)))
        w8 = jnp.pad(s["ik32"][:, IDX_DIM:IDX_DIM + IDX_HEADS], ((0, 0), (0, 8 - IDX_HEADS)))[:, :, None]
        scores = _idx_sample(page_table, qi8, w8, s["ik32"].reshape(n_s, 1, LANES), c_ik, l)
        bias = _select(scores.reshape(n_s, -1), top_k_sample, past_len + 1).reshape(n_s, 1, -1)
        oa8 = _dsa_sample(page_table, s["qa"].reshape(n_s, A_HEADS, LANES), s["ka32"].reshape(n_s, 1, -1),
                          s["va32"].reshape(n_s, 1, -1), bias, c_ak, c_av, l)
        rep = A_HEADS // A_KV_HEADS
        oa_s = jnp.concatenate([oa8[:, h, 64 * (h // rep):64 * (h // rep) + 64] for h in range(A_HEADS)],
                               axis=1).astype(BF16)
        ob_s = _diff_sample(page_table, s["bq"].reshape(n_s, 2 * B_HEADS, LANES), s["bk32"].reshape(n_s, 1, -1),
                            s["bv32"].reshape(n_s, 1, -1), c_bk, c_bv, l, lam_params + (subln_gain[l],),
                            lambda_init).reshape(n_s, -1)
        om_s = _mem_sample(xs, norm_mix[l], w["mq"], c_mk, c_mv, l).reshape(n_s, -1)
        xs = _post(xs, norm_mix[l], oa_s, ob_s, w, om=om_s)
        if last:
            xs, ys = _ffn(xs, norm_ffn2[l], f2_in, f2_out, final_norm)
        else:
            xs = _ffn(xs, norm_ffn2[l], f2_in, f2_out)

    def stack(rows, name, n, t, shape):
        width = int(math.prod(shape))
        return jnp.stack([r[name][:, :width].reshape(n, t, *shape) for r in rows], axis=1)

    def stack_fm(name, feat, shape):
        a = jnp.stack([r[name][:, :feat, :] for r in rows_p], axis=1)
        return jnp.moveaxis(a.reshape(n_b, depth, *shape, seq), -1, 2)

    def stack_mem(i):
        return jnp.stack([r[i].reshape(n_b, MEM_TOKENS, M_HEADS, M_HEAD_DIM) for r in mem_rows_out], axis=1)

    kv_p = (stack_fm("kat32", LANES, (A_KV_HEADS, A_HEAD_DIM)), stack_fm("vat32", LANES, (A_KV_HEADS, A_HEAD_DIM)),
            stack_fm("ikt32", IDX_DIM, (IDX_DIM,)),
            stack(rows_p, "bk32", n_b, seq, (B_HEADS, 2 * B_QK_DIM)), stack(rows_p, "bv32", n_b, seq, (B_HEADS, B_V_DIM)))
    kv_s = (stack(rows_s, "ka32", n_s, dec_seq, (A_KV_HEADS, A_HEAD_DIM)),
            stack(rows_s, "va32", n_s, dec_seq, (A_KV_HEADS, A_HEAD_DIM)),
            stack(rows_s, "ik32", n_s, dec_seq, (IDX_DIM,)),
            stack(rows_s, "bk32", n_s, dec_seq, (B_HEADS, 2 * B_QK_DIM)),
            stack(rows_s, "bv32", n_s, dec_seq, (B_HEADS, B_V_DIM)))
    return ((yp.reshape(n_b, seq, d), ys.reshape(n_s, dec_seq, d)) + kv_p + (stack_mem(0), stack_mem(1)) + kv_s)
```
